```python
import jax, jax.numpy as jnp
from jax import lax
import numpy as np

D_MODEL = 4096
BATCH = 2
SEQ = 8192
DEPTH = 2

CHUNK = 64
QBLOCK = 128
CONV_WIDTH = 3
D_CONV = D_MODEL // 2
N_HEADS_B = 8
HEAD_DIM_QK = D_MODEL // 32
HEAD_DIM_V = 2 * HEAD_DIM_QK
D_ATTN = N_HEADS_B * HEAD_DIM_V
D_QK = N_HEADS_B * 2 * HEAD_DIM_QK
D_FF = ((8 * D_MODEL // 3 + 255) // 256) * 256
D_IN = 3 * D_CONV + 2 * D_QK + D_ATTN + 2 * D_MODEL
EPS = 1e-6

kernel_name = "hybrid_shortconv_diffattn_convffn_block"


def rmsnorm(x, g):
    xf = x.astype(jnp.float32)
    y = xf * lax.rsqrt(jnp.mean(xf * xf, axis=-1, keepdims=True) + EPS)
    return (y * g.astype(jnp.float32)).astype(x.dtype)


def causal_dwconv(x, w):
    s = x.shape[1]
    xp = jnp.pad(x, ((0, 0), (CONV_WIDTH - 1, 0), (0, 0)))
    return sum(w[k] * xp[:, k:k + s] for k in range(CONV_WIDTH))


def alibi_slopes(n_heads):
    return 2.0 ** (-8.0 * jnp.arange(1, n_heads + 1, dtype=jnp.float32) / n_heads)


def lambda_init(layer_idx):
    return 0.8 - 0.6 * float(np.exp(-0.3 * layer_idx))


def diff_attention(q, k, v, lam, lam_init, subln_g):
    b, s = q.shape[0], q.shape[1]
    nb = s // QBLOCK
    scale = HEAD_DIM_QK ** -0.5
    slopes = alibi_slopes(N_HEADS_B)
    key_pos = jnp.arange(s)
    q_blocks = q.reshape(b, nb, QBLOCK, N_HEADS_B, 2, HEAD_DIM_QK).transpose(1, 0, 2, 3, 4, 5)
    starts = jnp.arange(nb) * QBLOCK

    def one_block(args):
        q_blk, start = args
        qry_pos = start + jnp.arange(QBLOCK)
        scores = jnp.einsum('bqhcd,bkhcd->bchqk', q_blk, k,
                            preferred_element_type=jnp.float32) * scale
        dist = jnp.abs(qry_pos[:, None] - key_pos[None, :]).astype(jnp.float32)
        bias = -slopes[:, None, None] * dist[None]
        visible = (key_pos[None, :] // CHUNK) <= (qry_pos[:, None] // CHUNK)
        scores = jnp.where(visible[None, None, None], scores + bias[None, None], -1e30)
        p = jax.nn.softmax(scores, axis=-1)
        a = p[:, 0] - lam * p[:, 1]
        return jnp.einsum('bhqk,bkhd->bqhd', a.astype(v.dtype), v)

    out = lax.map(one_block, (q_blocks, starts))
    out = out.transpose(1, 0, 2, 3, 4).reshape(b, s, N_HEADS_B, HEAD_DIM_V)
    out = rmsnorm(out, subln_g) * (1.0 - lam_init)
    return out.reshape(b, s, D_ATTN)


def setup_inputs(seed: int = 0) -> dict:
    key = jax.random.key(seed)
    ks = jax.random.split(key, 20)
    f32 = jnp.float32

    def w(k, shape, fan_in):
        return jax.random.normal(k, shape, f32) * fan_in ** -0.5

    def gain(k, shape):
        return 1.0 + 0.02 * jax.random.normal(k, shape, f32)

    return {
        "x": jax.random.normal(ks[0], (BATCH, SEQ, D_MODEL), f32),
        "w_in": w(ks[1], (DEPTH, D_MODEL, D_IN), D_MODEL),
        "b_gate": 0.01 * jax.random.normal(ks[2], (DEPTH, 2 * D_MODEL), f32),
        "conv_a": w(ks[3], (DEPTH, CONV_WIDTH, D_CONV), CONV_WIDTH),
        "w_a_out": w(ks[4], (DEPTH, D_CONV, D_MODEL), D_CONV),
        "lam_q1": 0.1 * jax.random.normal(ks[5], (DEPTH, HEAD_DIM_QK), f32),
        "lam_k1": 0.1 * jax.random.normal(ks[6], (DEPTH, HEAD_DIM_QK), f32),
        "lam_q2": 0.1 * jax.random.normal(ks[7], (DEPTH, HEAD_DIM_QK), f32),
        "lam_k2": 0.1 * jax.random.normal(ks[8], (DEPTH, HEAD_DIM_QK), f32),
        "subln_g": gain(ks[9], (DEPTH, HEAD_DIM_V)),
        "w_b_out": w(ks[10], (DEPTH, D_ATTN, D_MODEL), D_ATTN),
        "w_o": w(ks[11], (DEPTH, D_MODEL, D_MODEL), D_MODEL),
        "norm_mix_pre": gain(ks[12], (DEPTH, D_MODEL)),
        "norm_mix_post": gain(ks[13], (DEPTH, D_MODEL)),
        "w_ffn_in": w(ks[14], (DEPTH, D_MODEL, 2 * D_FF), D_MODEL),
        "conv_ffn": w(ks[15], (DEPTH, CONV_WIDTH, D_FF), CONV_WIDTH),
        "w_ffn_out": w(ks[16], (DEPTH, D_FF, D_MODEL), D_FF),
        "norm_ffn_pre": gain(ks[17], (DEPTH, D_MODEL)),
        "norm_ffn_post": gain(ks[18], (DEPTH, D_MODEL)),
    }


def reference(x, w_in, b_gate, conv_a, w_a_out, lam_q1, lam_k1, lam_q2, lam_k2, subln_g,
              w_b_out, w_o, norm_mix_pre, norm_mix_post, w_ffn_in, conv_ffn, w_ffn_out,
              norm_ffn_pre, norm_ffn_post):
    b, s, _ = x.shape
    for l in range(DEPTH):
        h = rmsnorm(x, norm_mix_pre[l])
        proj = h @ w_in[l]
        o = 0
        a_in = proj[..., o:o + D_CONV]; o += D_CONV
        a_b = proj[..., o:o + D_CONV]; o += D_CONV
        a_c = proj[..., o:o + D_CONV]; o += D_CONV
        q = proj[..., o:o + D_QK].reshape(b, s, N_HEADS_B, 2, HEAD_DIM_QK); o += D_QK
        k = proj[..., o:o + D_QK].reshape(b, s, N_HEADS_B, 2, HEAD_DIM_QK); o += D_QK
        v = proj[..., o:o + D_ATTN].reshape(b, s, N_HEADS_B, HEAD_DIM_V); o += D_ATTN
        gates = jax.nn.sigmoid(proj[..., o:o + 2 * D_MODEL] + b_gate[l])
        g_a, g_b = gates[..., :D_MODEL], gates[..., D_MODEL:]

        y_a = (a_b * causal_dwconv(a_c * a_in, conv_a[l])) @ w_a_out[l]

        lam_init = lambda_init(l)
        lam = (jnp.exp(jnp.sum(lam_q1[l].astype(jnp.float32) * lam_k1[l].astype(jnp.float32)))
               - jnp.exp(jnp.sum(lam_q2[l].astype(jnp.float32) * lam_k2[l].astype(jnp.float32)))
               + lam_init)
        y_b = diff_attention(q, k, v, lam, lam_init, subln_g[l]) @ w_b_out[l]

        mix = (g_a * y_a + g_b * y_b) @ w_o[l]
        x = x + rmsnorm(mix, norm_mix_post[l])

        h2 = rmsnorm(x, norm_ffn_pre[l])
        up2 = h2 @ w_ffn_in[l]
        gate_f, up_f = up2[..., :D_FF], up2[..., D_FF:]
        act = jax.nn.gelu(causal_dwconv(gate_f, conv_ffn[l]), approximate=True)
        ffn = (act * up_f) @ w_ffn_out[l]
        x = x + rmsnorm(ffn, norm_ffn_post[l])
    return x
```

```python
import functools
import math

import jax
import jax.numpy as jnp
from jax import lax
from jax.experimental import pallas as pl
from jax.experimental.pallas import tpu as pltpu

CHUNK = 64
CONV_WIDTH = 3
N_HEADS = 8
HEAD_DIM_QK = 128
HEAD_DIM_V = 256
EPS = 1e-6
NEG_INF = -1e30

VMEM_LIMIT_BYTES = 56 * 1024 * 1024
HALO_ROWS = 16

F32 = jnp.float32
BF16 = jnp.bfloat16


def _params(n_grid_dims):
    return pltpu.CompilerParams(
        dimension_semantics=("arbitrary",) * n_grid_dims,
        vmem_limit_bytes=VMEM_LIMIT_BYTES)


def _rms(x, g):
    return x * lax.rsqrt(jnp.mean(x * x, axis=-1, keepdims=True) + EPS) * g


def _norm_kernel(x_ref, g_ref, h_ref):
    h_ref[...] = _rms(x_ref[...], g_ref[...]).astype(h_ref.dtype)


def _norm(x, g, *, rows=256):
    t, d = x.shape
    return pl.pallas_call(
        _norm_kernel,
        grid=(t // rows,),
        in_specs=[pl.BlockSpec((rows, d), lambda i: (i, 0)),
                  pl.BlockSpec((1, d), lambda i: (0, 0))],
        out_specs=pl.BlockSpec((rows, d), lambda i: (i, 0)),
        out_shape=jax.ShapeDtypeStruct((t, d), BF16),
        compiler_params=_params(1),
        name="norm",
    )(x, g.reshape(1, d))


def _resid_norm_kernel(x_ref, y_ref, gpost_ref, gpre_ref, xo_ref, h_ref):
    x_new = x_ref[...] + _rms(y_ref[...], gpost_ref[...])
    xo_ref[...] = x_new
    h_ref[...] = _rms(x_new, gpre_ref[...]).astype(h_ref.dtype)


def _resid_kernel(x_ref, y_ref, gpost_ref, xo_ref):
    xo_ref[...] = x_ref[...] + _rms(y_ref[...], gpost_ref[...])


def _resid_norm(x, y, g_post, g_pre_next, *, rows=256):
    t, d = x.shape
    row_spec = pl.BlockSpec((rows, d), lambda i: (i, 0))
    vec_spec = pl.BlockSpec((1, d), lambda i: (0, 0))
    if g_pre_next is None:
        return pl.pallas_call(
            _resid_kernel,
            grid=(t // rows,),
            in_specs=[row_spec, row_spec, vec_spec],
            out_specs=row_spec,
            out_shape=jax.ShapeDtypeStruct((t, d), F32),
            compiler_params=_params(1),
            name="resid",
        )(x, y, g_post.reshape(1, d)), None
    return pl.pallas_call(
        _resid_norm_kernel,
        grid=(t // rows,),
        in_specs=[row_spec, row_spec, vec_spec, vec_spec],
        out_specs=[row_spec, row_spec],
        out_shape=[jax.ShapeDtypeStruct((t, d), F32), jax.ShapeDtypeStruct((t, d), BF16)],
        compiler_params=_params(1),
        name="resid_norm",
    )(x, y, g_post.reshape(1, d), g_pre_next.reshape(1, d))


def _mm_kernel(a_ref, w_ref, o_ref):
    o_ref[...] = jnp.dot(a_ref[...], w_ref[...], preferred_element_type=F32).astype(o_ref.dtype)


def _matmul(a, w, out_dtype, *, tm, tn, name):
    t, k = a.shape
    n = w.shape[1]
    return pl.pallas_call(
        _mm_kernel,
        grid=(n // tn, t // tm),
        in_specs=[pl.BlockSpec((tm, k), lambda j, i: (i, 0)),
                  pl.BlockSpec((k, tn), lambda j, i: (0, j))],
        out_specs=pl.BlockSpec((tm, tn), lambda j, i: (i, j)),
        out_shape=jax.ShapeDtypeStruct((t, n), out_dtype),
        compiler_params=_params(2),
        name=name,
    )(a, w)


def _causal_conv3(u, halo, w_ref):
    rows = u.shape[0]
    ext = jnp.concatenate([halo, u], axis=0)
    u1 = pltpu.roll(ext, 1, 0)[HALO_ROWS:HALO_ROWS + rows]
    u2 = pltpu.roll(ext, 2, 0)[HALO_ROWS:HALO_ROWS + rows]
    return w_ref[0:1, :] * u2 + w_ref[1:2, :] * u1 + w_ref[2:3, :] * u


def _halo_scale(rows, seq):
    at_start = (pl.program_id(0) * rows) % seq == 0
    return jnp.where(at_start, 0.0, 1.0).astype(F32)


def _conv_a_kernel(ain_ref, ab_ref, ac_ref, ain_h_ref, ac_h_ref, w_ref, o_ref, *, rows, seq):
    u = ac_ref[...].astype(F32) * ain_ref[...].astype(F32)
    halo = ac_h_ref[...].astype(F32) * ain_h_ref[...].astype(F32) * _halo_scale(rows, seq)
    y = ab_ref[...].astype(F32) * _causal_conv3(u, halo, w_ref)
    o_ref[...] = y.astype(o_ref.dtype)


def _conv_a(proj, conv_w, *, seq, d_conv, rows=512, cols=512):
    t = proj.shape[0]
    ncb = d_conv // cols
    hb = rows // HALO_ROWS

    def sect(s):
        return pl.BlockSpec((rows, cols), lambda i, j: (i, s * ncb + j))

    def halo(s):
        return pl.BlockSpec((HALO_ROWS, cols),
                            lambda i, j: (jnp.maximum(i * hb - 1, 0), s * ncb + j))

    return pl.pallas_call(
        functools.partial(_conv_a_kernel, rows=rows, seq=seq),
        grid=(t // rows, ncb),
        in_specs=[sect(0), sect(1), sect(2), halo(0), halo(2),
                  pl.BlockSpec((CONV_WIDTH, cols), lambda i, j: (0, j))],
        out_specs=pl.BlockSpec((rows, cols), lambda i, j: (i, j)),
        out_shape=jax.ShapeDtypeStruct((t, d_conv), BF16),
        compiler_params=_params(2),
        name="conv_a",
    )(proj, proj, proj, proj, proj, conv_w)


def _attn_kernel(slopes_ref, lq1_ref, lk1_ref, lq2_ref, lk2_ref, g_ref, q_ref, k_ref, v_ref,
                 o_ref, m_ref, l_ref, acc_ref, *, lam_init, blk):
    h = pl.program_id(1)
    qi = pl.program_id(2)
    slope = slopes_ref[h]
    scale = HEAD_DIM_QK ** -0.5
    q_start = qi * blk

    m_ref[...] = jnp.full(m_ref.shape, NEG_INF, F32)
    l_ref[...] = jnp.zeros(l_ref.shape, F32)
    acc_ref[...] = jnp.zeros(acc_ref.shape, F32)

    def attend(j, bias_fn):
        k = k_ref[pl.ds(pl.multiple_of(j * blk, blk), blk), :]
        v = v_ref[pl.ds(pl.multiple_of(j * blk, blk), blk), :]
        for c in range(2):
            qc = q_ref[:, c * HEAD_DIM_QK:(c + 1) * HEAD_DIM_QK]
            kc = k[:, c * HEAD_DIM_QK:(c + 1) * HEAD_DIM_QK]
            s = lax.dot_general(qc, kc, (((1,), (1,)), ((), ())), preferred_element_type=F32)
            s = bias_fn(s * scale)
            m_old = m_ref[c]
            m_new = jnp.maximum(m_old, jnp.max(s, axis=-1, keepdims=True))
            alpha = jnp.exp(m_old - m_new)
            p = jnp.exp(s - m_new)
            l_ref[c] = alpha * l_ref[c] + jnp.sum(p, axis=-1, keepdims=True)
            acc_ref[c] = alpha * acc_ref[c] + jnp.dot(p.astype(v.dtype), v, preferred_element_type=F32)
            m_ref[c] = m_new

    def past_block(j, carry):
        kpos = j * blk + lax.broadcasted_iota(jnp.int32, (1, blk), 1)
        bias = slope * (kpos - q_start).astype(F32)
        attend(j, lambda s: s + bias)
        return carry

    lax.fori_loop(0, qi, past_block, 0)

    def diag_bias(s):
        qrel = lax.broadcasted_iota(jnp.int32, (blk, blk), 0)
        krel = lax.broadcasted_iota(jnp.int32, (blk, blk), 1)
        bias = slope * (qrel - jnp.abs(qrel - krel)).astype(F32)
        visible = (krel // CHUNK) <= (qrel // CHUNK)
        return jnp.where(visible, s + bias, NEG_INF)

    attend(qi, diag_bias)

    lam = (jnp.exp(jnp.sum(lq1_ref[...] * lk1_ref[...], axis=-1, keepdims=True))
           - jnp.exp(jnp.sum(lq2_ref[...] * lk2_ref[...], axis=-1, keepdims=True))
           + lam_init)
    o = acc_ref[0] / l_ref[0] - lam * (acc_ref[1] / l_ref[1])
    o_ref[...] = (_rms(o, g_ref[...]) * (1.0 - lam_init)).astype(o_ref.dtype)


def _diff_attention(proj, lam_q1, lam_k1, lam_q2, lam_k2, subln_g, *, batch, seq, q_col, k_col,
                    v_col, lam_init, blk=512):
    t = proj.shape[0]
    hw = 2 * HEAD_DIM_QK
    assert hw == HEAD_DIM_V and seq % blk == 0 and blk % CHUNK == 0
    nq = seq // blk
    slopes = 2.0 ** (-8.0 * jnp.arange(1, N_HEADS + 1, dtype=F32) / N_HEADS)
    vec = lambda n: pl.BlockSpec((1, n), lambda b, h, i, s: (0, 0))
    grid_spec = pltpu.PrefetchScalarGridSpec(
        num_scalar_prefetch=1,
        grid=(batch, N_HEADS, nq),
        in_specs=[vec(HEAD_DIM_QK), vec(HEAD_DIM_QK), vec(HEAD_DIM_QK), vec(HEAD_DIM_QK),
                  vec(HEAD_DIM_V),
                  pl.BlockSpec((blk, hw), lambda b, h, i, s: (b * nq + i, q_col // hw + h)),
                  pl.BlockSpec((seq, hw), lambda b, h, i, s: (b, k_col // hw + h)),
                  pl.BlockSpec((seq, HEAD_DIM_V), lambda b, h, i, s: (b, v_col // HEAD_DIM_V + h))],
        out_specs=pl.BlockSpec((blk, HEAD_DIM_V), lambda b, h, i, s: (b * nq + i, h)),
        scratch_shapes=[pltpu.VMEM((2, blk, 1), F32), pltpu.VMEM((2, blk, 1), F32),
                        pltpu.VMEM((2, blk, HEAD_DIM_V), F32)],
    )
    return pl.pallas_call(
        functools.partial(_attn_kernel, lam_init=lam_init, blk=blk),
        grid_spec=grid_spec,
        out_shape=jax.ShapeDtypeStruct((t, N_HEADS * HEAD_DIM_V), BF16),
        compiler_params=_params(3),
        name="diff_attn",
    )(slopes, lam_q1.reshape(1, -1), lam_k1.reshape(1, -1), lam_q2.reshape(1, -1),
      lam_k2.reshape(1, -1), subln_g.reshape(1, -1), proj, proj, proj)


def _merge_kernel(ca_ref, at_ref, wa_ref, wb_ref, ga_ref, gb_ref, ba_ref, bb_ref, o_ref):
    y_a = jnp.dot(ca_ref[...], wa_ref[...], preferred_element_type=F32)
    y_b = jnp.dot(at_ref[...], wb_ref[...], preferred_element_type=F32)
    g_a = jax.nn.sigmoid(ga_ref[...].astype(F32) + ba_ref[...])
    g_b = jax.nn.sigmoid(gb_ref[...].astype(F32) + bb_ref[...])
    o_ref[...] = (g_a * y_a + g_b * y_b).astype(o_ref.dtype)


def _merge(ca, attn, w_a, w_b, proj, b_gate, *, gate_col, d_model, tm=1024, tn=1024):
    t, ka = ca.shape
    kb = attn.shape[1]
    gcb = gate_col // tn
    ncb = d_model // tn
    bias = b_gate.reshape(1, -1)
    return pl.pallas_call(
        _merge_kernel,
        grid=(ncb, t // tm),
        in_specs=[pl.BlockSpec((tm, ka), lambda j, i: (i, 0)),
                  pl.BlockSpec((tm, kb), lambda j, i: (i, 0)),
                  pl.BlockSpec((ka, tn), lambda j, i: (0, j)),
                  pl.BlockSpec((kb, tn), lambda j, i: (0, j)),
                  pl.BlockSpec((tm, tn), lambda j, i: (i, gcb + j)),
                  pl.BlockSpec((tm, tn), lambda j, i: (i, gcb + ncb + j)),
                  pl.BlockSpec((1, tn), lambda j, i: (0, j)),
                  pl.BlockSpec((1, tn), lambda j, i: (0, ncb + j))],
        out_specs=pl.BlockSpec((tm, tn), lambda j, i: (i, j)),
        out_shape=jax.ShapeDtypeStruct((t, d_model), BF16),
        compiler_params=_params(2),
        name="merge",
    )(ca, attn, w_a, w_b, proj, proj, bias, bias)


def _ffn_gate_kernel(g_ref, u_ref, g_h_ref, w_ref, o_ref, *, rows, seq):
    gate = g_ref[...].astype(F32)
    halo = g_h_ref[...].astype(F32) * _halo_scale(rows, seq)
    z = _causal_conv3(gate, halo, w_ref)
    c0 = math.sqrt(2.0 / math.pi)
    act = 0.5 * z * (1.0 + jnp.tanh(c0 * (z + 0.044715 * (z * z * z))))
    o_ref[...] = (act * u_ref[...].astype(F32)).astype(o_ref.dtype)


def _ffn_gate(up2, conv_w, *, seq, d_ff, rows=256, cols=None):
    t = up2.shape[0]
    cols = cols or d_ff // 2
    ncb = d_ff // cols
    hb = rows // HALO_ROWS
    return pl.pallas_call(
        functools.partial(_ffn_gate_kernel, rows=rows, seq=seq),
        grid=(t // rows, ncb),
        in_specs=[pl.BlockSpec((rows, cols), lambda i, j: (i, j)),
                  pl.BlockSpec((rows, cols), lambda i, j: (i, ncb + j)),
                  pl.BlockSpec((HALO_ROWS, cols), lambda i, j: (jnp.maximum(i * hb - 1, 0), j)),
                  pl.BlockSpec((CONV_WIDTH, cols), lambda i, j: (0, j))],
        out_specs=pl.BlockSpec((rows, cols), lambda i, j: (i, j)),
        out_shape=jax.ShapeDtypeStruct((t, d_ff), BF16),
        compiler_params=_params(2),
        name="ffn_gate",
    )(up2, up2, up2, conv_w)


def _lambda_init(layer_idx):
    return 0.8 - 0.6 * math.exp(-0.3 * layer_idx)


def kernel(x, w_in, b_gate, conv_a, w_a_out, lam_q1, lam_k1, lam_q2, lam_k2, subln_g, w_b_out, w_o,
           norm_mix_pre, norm_mix_post, w_ffn_in, conv_ffn, w_ffn_out, norm_ffn_pre, norm_ffn_post):
    batch, seq, d_model = x.shape
    depth = w_in.shape[0]
    d_conv = conv_a.shape[-1]
    d_ff = conv_ffn.shape[-1]
    d_qk = N_HEADS * 2 * HEAD_DIM_QK
    q_col = 3 * d_conv
    k_col = q_col + d_qk
    v_col = k_col + d_qk
    gate_col = v_col + N_HEADS * HEAD_DIM_V

    xt = x.reshape(batch * seq, d_model)
    h = _norm(xt, norm_mix_pre[0])
    for l in range(depth):
        proj = _matmul(h, w_in[l].astype(BF16), BF16, tm=1024, tn=1024, name="in_proj")
        ca = _conv_a(proj, conv_a[l], seq=seq, d_conv=d_conv)
        attn = _diff_attention(proj, lam_q1[l], lam_k1[l], lam_q2[l], lam_k2[l], subln_g[l],
                               batch=batch, seq=seq, q_col=q_col, k_col=k_col, v_col=v_col,
                               lam_init=_lambda_init(l))
        merged = _merge(ca, attn, w_a_out[l].astype(BF16), w_b_out[l].astype(BF16), proj, b_gate[l],
                        gate_col=gate_col, d_model=d_model)
        mix = _matmul(merged, w_o[l].astype(BF16), F32, tm=1024, tn=1024, name="w_o")
        xt, h2 = _resid_norm(xt, mix, norm_mix_post[l], norm_ffn_pre[l])

        up2 = _matmul(h2, w_ffn_in[l].astype(BF16), BF16, tm=1024, tn=512, name="ffn_in")
        f = _ffn_gate(up2, conv_ffn[l], seq=seq, d_ff=d_ff)
        ffn = _matmul(f, w_ffn_out[l].astype(BF16), F32, tm=512, tn=512, name="ffn_out")
        g_next = norm_mix_pre[l + 1] if l + 1 < depth else None
        xt, h = _resid_norm(xt, ffn, norm_ffn_post[l], g_next)
    return xt.reshape(batch, seq, d_model)
```

```python
import functools
import math

import jax
import jax.numpy as jnp
from jax import lax
from jax.experimental import pallas as pl
from jax.experimental.pallas import tpu as pltpu

CHUNK = 64
CONV_WIDTH = 3
N_HEADS = 8
HEAD_DIM_QK = 128
HEAD_DIM_V = 256
EPS = 1e-6
NEG_INF = -1e30
LOG2E = math.log2(math.e)
QK_SCALE_LOG2 = HEAD_DIM_QK ** -0.5 * LOG2E

VMEM_LIMIT_BYTES = 56 * 1024 * 1024
HALO_ROWS = 16

F32 = jnp.float32
BF16 = jnp.bfloat16


def _params(n_grid_dims):
    return pltpu.CompilerParams(
        dimension_semantics=("arbitrary",) * n_grid_dims,
        vmem_limit_bytes=VMEM_LIMIT_BYTES)


def _rms(x, g):
    return x * lax.rsqrt(jnp.mean(x * x, axis=-1, keepdims=True) + EPS) * g


def _norm_kernel(x_ref, g_ref, h_ref):
    h_ref[...] = _rms(x_ref[...], g_ref[...]).astype(h_ref.dtype)


def _norm(x, g, *, rows=256):
    t, d = x.shape
    return pl.pallas_call(
        _norm_kernel,
        grid=(t // rows,),
        in_specs=[pl.BlockSpec((rows, d), lambda i: (i, 0)),
                  pl.BlockSpec((1, d), lambda i: (0, 0))],
        out_specs=pl.BlockSpec((rows, d), lambda i: (i, 0)),
        out_shape=jax.ShapeDtypeStruct((t, d), BF16),
        compiler_params=_params(1),
        name="norm",
    )(x, g.reshape(1, d))


def _resid_norm_kernel(x_ref, y_ref, gpost_ref, gpre_ref, xo_ref, h_ref):
    x_new = x_ref[...] + _rms(y_ref[...], gpost_ref[...])
    xo_ref[...] = x_new
    h_ref[...] = _rms(x_new, gpre_ref[...]).astype(h_ref.dtype)


def _resid_kernel(x_ref, y_ref, gpost_ref, xo_ref):
    xo_ref[...] = x_ref[...] + _rms(y_ref[...], gpost_ref[...])


def _resid_norm(x, y, g_post, g_pre_next, *, rows=256):
    t, d = x.shape
    row_spec = pl.BlockSpec((rows, d), lambda i: (i, 0))
    vec_spec = pl.BlockSpec((1, d), lambda i: (0, 0))
    if g_pre_next is None:
        return pl.pallas_call(
            _resid_kernel,
            grid=(t // rows,),
            in_specs=[row_spec, row_spec, vec_spec],
            out_specs=row_spec,
            out_shape=jax.ShapeDtypeStruct((t, d), F32),
            compiler_params=_params(1),
            name="resid",
        )(x, y, g_post.reshape(1, d)), None
    return pl.pallas_call(
        _resid_norm_kernel,
        grid=(t // rows,),
        in_specs=[row_spec, row_spec, vec_spec, vec_spec],
        out_specs=[row_spec, row_spec],
        out_shape=[jax.ShapeDtypeStruct((t, d), F32), jax.ShapeDtypeStruct((t, d), BF16)],
        compiler_params=_params(1),
        name="resid_norm",
    )(x, y, g_post.reshape(1, d), g_pre_next.reshape(1, d))


def _cast_weight_once(w_ref, wbf_ref):
    @pl.when(pl.program_id(1) == 0)
    def _():
        wbf_ref[...] = w_ref[...].astype(wbf_ref.dtype)


def _mm_direct_kernel(a_ref, w_ref, o_ref):
    o_ref[...] = jnp.dot(a_ref[...], w_ref[...], preferred_element_type=F32).astype(o_ref.dtype)


def _mm_kernel(a_ref, w_ref, o_ref, wbf_ref):
    _cast_weight_once(w_ref, wbf_ref)
    o_ref[...] = jnp.dot(a_ref[...], wbf_ref[...], preferred_element_type=F32).astype(o_ref.dtype)


def _mm_colscale_kernel(a_ref, w_ref, s_ref, o_ref, wbf_ref):
    _cast_weight_once(w_ref, wbf_ref)
    acc = jnp.dot(a_ref[...], wbf_ref[...], preferred_element_type=F32)
    o_ref[...] = (acc * s_ref[...]).astype(o_ref.dtype)


def _matmul(a, w, layer, out_dtype, *, tm, tn, name, col_scale=None, w_buffers=2):
    t, k = a.shape
    n = w.shape[2]
    in_specs = [pl.BlockSpec((tm, k), lambda j, i: (i, 0)),
                pl.BlockSpec((None, k, tn), lambda j, i: (layer, 0, j),
                             pipeline_mode=pl.Buffered(w_buffers))]
    operands = [a, w]
    if col_scale is not None:
        in_specs.append(pl.BlockSpec((1, tn), lambda j, i: (0, j)))
        operands.append(col_scale.reshape(1, n))
    if w.dtype == BF16:
        assert col_scale is None
        body, scratch = _mm_direct_kernel, []
    else:
        body = _mm_kernel if col_scale is None else _mm_colscale_kernel
        scratch = [pltpu.VMEM((k, tn), BF16)]
    return pl.pallas_call(
        body,
        grid=(n // tn, t // tm),
        in_specs=in_specs,
        out_specs=pl.BlockSpec((tm, tn), lambda j, i: (i, j)),
        out_shape=jax.ShapeDtypeStruct((t, n), out_dtype),
        scratch_shapes=scratch,
        compiler_params=_params(2),
        name=name,
    )(*operands)


def _causal_conv3(u, halo, w_ref):
    rows = u.shape[0]
    ext = jnp.concatenate([halo, u], axis=0)
    u1 = pltpu.roll(ext, 1, 0)[HALO_ROWS:HALO_ROWS + rows]
    u2 = pltpu.roll(ext, 2, 0)[HALO_ROWS:HALO_ROWS + rows]
    return w_ref[0:1, :] * u2 + w_ref[1:2, :] * u1 + w_ref[2:3, :] * u


def _halo_scale(rows, seq):
    at_start = (pl.program_id(0) * rows) % seq == 0
    return jnp.where(at_start, 0.0, 1.0).astype(F32)


def _conv_a_kernel(ain_ref, ab_ref, ac_ref, ain_h_ref, ac_h_ref, w_ref, o_ref, *, rows, seq):
    u = ac_ref[...].astype(F32) * ain_ref[...].astype(F32)
    halo = ac_h_ref[...].astype(F32) * ain_h_ref[...].astype(F32) * _halo_scale(rows, seq)
    y = ab_ref[...].astype(F32) * _causal_conv3(u, halo, w_ref)
    o_ref[...] = y.astype(o_ref.dtype)


def _conv_a(proj, conv_w, *, seq, d_conv, rows=512, cols=512):
    t = proj.shape[0]
    ncb = d_conv // cols
    hb = rows // HALO_ROWS

    def sect(s):
        return pl.BlockSpec((rows, cols), lambda i, j: (i, s * ncb + j))

    def halo(s):
        return pl.BlockSpec((HALO_ROWS, cols),
                            lambda i, j: (jnp.maximum(i * hb - 1, 0), s * ncb + j))

    return pl.pallas_call(
        functools.partial(_conv_a_kernel, rows=rows, seq=seq),
        grid=(t // rows, ncb),
        in_specs=[sect(0), sect(1), sect(2), halo(0), halo(2),
                  pl.BlockSpec((CONV_WIDTH, cols), lambda i, j: (0, j))],
        out_specs=pl.BlockSpec((rows, cols), lambda i, j: (i, j)),
        out_shape=jax.ShapeDtypeStruct((t, d_conv), BF16),
        compiler_params=_params(2),
        name="conv_a",
    )(proj, proj, proj, proj, proj, conv_w)


def _attn_kernel(slopes_ref, lq1_ref, lk1_ref, lq2_ref, lk2_ref, g_ref, q_ref, k_ref, v_ref,
                 o_ref, vt_ref, bias_ref, diag_ref, m_ref, l_ref, acc_ref, *, lam_init, blk, seq):
    h = pl.program_id(1)
    qi = pl.program_id(2)
    slope2 = slopes_ref[h] * LOG2E

    @pl.when(qi == 0)
    def _per_head_setup():
        def xpose(c, carry):
            rows = pl.ds(pl.multiple_of(c * blk, blk), blk)
            vt_ref[:, rows] = v_ref[rows, :].T
            return carry
        lax.fori_loop(0, seq // blk, xpose, 0)
        krel = lax.broadcasted_iota(jnp.int32, (blk, blk), 0)
        qrel = lax.broadcasted_iota(jnp.int32, (blk, blk), 1)
        bias_ref[...] = slope2 * krel.astype(F32)
        visible = (krel // CHUNK) <= (qrel // CHUNK)
        diag_ref[...] = jnp.where(visible, slope2 * (qrel - jnp.abs(qrel - krel)).astype(F32), NEG_INF)

    m_ref[...] = jnp.full(m_ref.shape, NEG_INF, F32)
    l_ref[...] = jnp.zeros(l_ref.shape, F32)
    acc_ref[...] = jnp.zeros(acc_ref.shape, F32)

    def attend(j, bias_tile_ref, off):
        rows = pl.ds(pl.multiple_of(j * blk, blk), blk)
        k = k_ref[rows, :]
        vt = vt_ref[:, rows]
        for c in range(2):
            kc = k[:, c * HEAD_DIM_QK:(c + 1) * HEAD_DIM_QK]
            qc = q_ref[:, c * HEAD_DIM_QK:(c + 1) * HEAD_DIM_QK]
            st = lax.dot_general(kc, qc, (((1,), (1,)), ((), ())), preferred_element_type=F32)
            u = st + bias_tile_ref[...]
            m_old = m_ref[c]
            m_new = jnp.maximum(m_old, jnp.max(u, axis=0, keepdims=True) + off)
            alpha = jnp.exp2(m_old - m_new)
            p = jnp.exp2(u - (m_new - off))
            l_ref[c] = alpha * l_ref[c] + jnp.sum(p, axis=0, keepdims=True)
            acc_ref[c] = alpha * acc_ref[c] + jnp.dot(vt, p.astype(vt.dtype), preferred_element_type=F32)
            m_ref[c] = m_new

    def past_block(j, carry):
        off = slope2 * jnp.full((1, blk), (j - qi) * blk, jnp.int32).astype(F32)
        attend(j, bias_ref, off)
        return carry

    lax.fori_loop(0, qi, past_block, 0)
    attend(qi, diag_ref, 0.0)

    lam = (jnp.exp(jnp.sum(lq1_ref[...] * lk1_ref[...], axis=-1, keepdims=True))
           - jnp.exp(jnp.sum(lq2_ref[...] * lk2_ref[...], axis=-1, keepdims=True))
           + lam_init)
    ot = acc_ref[0] * (1.0 / l_ref[0]) - acc_ref[1] * (lam / l_ref[1])
    ot = ot * lax.rsqrt(jnp.mean(ot * ot, axis=0, keepdims=True) + EPS)
    o_ref[...] = (ot.T * (g_ref[...] * (1.0 - lam_init))).astype(o_ref.dtype)


def _diff_attention(proj, lam_q1, lam_k1, lam_q2, lam_k2, subln_g, *, batch, seq, q_col, k_col,
                    v_col, lam_init, blk=512):
    t = proj.shape[0]
    hw = 2 * HEAD_DIM_QK
    assert hw == HEAD_DIM_V and seq % blk == 0 and blk % CHUNK == 0
    nq = seq // blk
    slopes = 2.0 ** (-8.0 * jnp.arange(1, N_HEADS + 1, dtype=F32) / N_HEADS)
    vec = lambda n: pl.BlockSpec((1, n), lambda b, h, i, s: (0, 0))
    grid_spec = pltpu.PrefetchScalarGridSpec(
        num_scalar_prefetch=1,
        grid=(batch, N_HEADS, nq),
        in_specs=[vec(HEAD_DIM_QK), vec(HEAD_DIM_QK), vec(HEAD_DIM_QK), vec(HEAD_DIM_QK),
                  vec(HEAD_DIM_V),
                  pl.BlockSpec((blk, hw), lambda b, h, i, s: (b * nq + i, q_col // hw + h)),
                  pl.BlockSpec((seq, hw), lambda b, h, i, s: (b, k_col // hw + h)),
                  pl.BlockSpec((seq, HEAD_DIM_V), lambda b, h, i, s: (b, v_col // HEAD_DIM_V + h))],
        out_specs=pl.BlockSpec((blk, HEAD_DIM_V), lambda b, h, i, s: (b * nq + i, h)),
        scratch_shapes=[pltpu.VMEM((HEAD_DIM_V, seq), BF16),
                        pltpu.VMEM((blk, blk), F32), pltpu.VMEM((blk, blk), F32),
                        pltpu.VMEM((2, 1, blk), F32), pltpu.VMEM((2, 1, blk), F32),
                        pltpu.VMEM((2, HEAD_DIM_V, blk), F32)],
    )
    return pl.pallas_call(
        functools.partial(_attn_kernel, lam_init=lam_init, blk=blk, seq=seq),
        grid_spec=grid_spec,
        out_shape=jax.ShapeDtypeStruct((t, N_HEADS * HEAD_DIM_V), BF16),
        compiler_params=_params(3),
        name="diff_attn",
    )(slopes, lam_q1.reshape(1, -1), lam_k1.reshape(1, -1), lam_q2.reshape(1, -1),
      lam_k2.reshape(1, -1), subln_g.reshape(1, -1), proj, proj, proj)


def _merge_kernel(ca_ref, at_ref, wa_ref, wb_ref, ga_ref, gb_ref, ba_ref, bb_ref, o_ref,
                  wabf_ref, wbbf_ref):
    _cast_weight_once(wa_ref, wabf_ref)
    _cast_weight_once(wb_ref, wbbf_ref)
    y_a = jnp.dot(ca_ref[...], wabf_ref[...], preferred_element_type=F32)
    y_b = jnp.dot(at_ref[...], wbbf_ref[...], preferred_element_type=F32)
    g_a = jax.nn.sigmoid(ga_ref[...].astype(F32) + ba_ref[...])
    g_b = jax.nn.sigmoid(gb_ref[...].astype(F32) + bb_ref[...])
    o_ref[...] = (g_a * y_a + g_b * y_b).astype(o_ref.dtype)


def _merge(ca, attn, w_a, w_b, layer, proj, b_gate, *, gate_col, d_model, tm=1024, tn=512):
    t, ka = ca.shape
    kb = attn.shape[1]
    gcb = gate_col // tn
    ncb = d_model // tn
    bias = b_gate.reshape(1, -1)
    return pl.pallas_call(
        _merge_kernel,
        grid=(ncb, t // tm),
        in_specs=[pl.BlockSpec((tm, ka), lambda j, i: (i, 0)),
                  pl.BlockSpec((tm, kb), lambda j, i: (i, 0)),
                  pl.BlockSpec((None, ka, tn), lambda j, i: (layer, 0, j)),
                  pl.BlockSpec((None, kb, tn), lambda j, i: (layer, 0, j)),
                  pl.BlockSpec((tm, tn), lambda j, i: (i, gcb + j)),
                  pl.BlockSpec((tm, tn), lambda j, i: (i, gcb + ncb + j)),
                  pl.BlockSpec((1, tn), lambda j, i: (0, j)),
                  pl.BlockSpec((1, tn), lambda j, i: (0, ncb + j))],
        out_specs=pl.BlockSpec((tm, tn), lambda j, i: (i, j)),
        out_shape=jax.ShapeDtypeStruct((t, d_model), BF16),
        scratch_shapes=[pltpu.VMEM((ka, tn), BF16), pltpu.VMEM((kb, tn), BF16)],
        compiler_params=_params(2),
        name="merge",
    )(ca, attn, w_a, w_b, proj, proj, bias, bias)


def _ffn_gate_kernel(g_ref, u_ref, g_h_ref, w_ref, o_ref, *, rows, seq):
    gate = g_ref[...].astype(F32)
    halo = g_h_ref[...].astype(F32) * _halo_scale(rows, seq)
    z = _causal_conv3(gate, halo, w_ref)
    c0 = math.sqrt(2.0 / math.pi)
    act = 0.5 * z * (1.0 + jnp.tanh(c0 * (z + 0.044715 * (z * z * z))))
    o_ref[...] = (act * u_ref[...].astype(F32)).astype(o_ref.dtype)


def _ffn_gate(up2, conv_w, *, seq, d_ff, rows=256, cols=None):
    t = up2.shape[0]
    cols = cols or d_ff // 2
    ncb = d_ff // cols
    hb = rows // HALO_ROWS
    return pl.pallas_call(
        functools.partial(_ffn_gate_kernel, rows=rows, seq=seq),
        grid=(t // rows, ncb),
        in_specs=[pl.BlockSpec((rows, cols), lambda i, j: (i, j)),
                  pl.BlockSpec((rows, cols), lambda i, j: (i, ncb + j)),
                  pl.BlockSpec((HALO_ROWS, cols), lambda i, j: (jnp.maximum(i * hb - 1, 0), j)),
                  pl.BlockSpec((CONV_WIDTH, cols), lambda i, j: (0, j))],
        out_specs=pl.BlockSpec((rows, cols), lambda i, j: (i, j)),
        out_shape=jax.ShapeDtypeStruct((t, d_ff), BF16),
        compiler_params=_params(2),
        name="ffn_gate",
    )(up2, up2, up2, conv_w)


def _lambda_init(layer_idx):
    return 0.8 - 0.6 * math.exp(-0.3 * layer_idx)


def kernel(x, w_in, b_gate, conv_a, w_a_out, lam_q1, lam_k1, lam_q2, lam_k2, subln_g, w_b_out, w_o,
           norm_mix_pre, norm_mix_post, w_ffn_in, conv_ffn, w_ffn_out, norm_ffn_pre, norm_ffn_post):
    batch, seq, d_model = x.shape
    depth = w_in.shape[0]
    d_conv = conv_a.shape[-1]
    d_ff = conv_ffn.shape[-1]
    d_qk = N_HEADS * 2 * HEAD_DIM_QK
    q_col = 3 * d_conv
    k_col = q_col + d_qk
    v_col = k_col + d_qk
    gate_col = v_col + N_HEADS * HEAD_DIM_V

    d_in = w_in.shape[-1]
    q_scale = jnp.ones((d_in,), F32).at[q_col:k_col].set(QK_SCALE_LOG2)

    w_ffn_out_bf = w_ffn_out.astype(BF16)

    xt = x.reshape(batch * seq, d_model)
    h = _norm(xt, norm_mix_pre[0])
    for l in range(depth):
        proj = _matmul(h, w_in, l, BF16, tm=1024, tn=1024, name="in_proj", col_scale=q_scale,
                       w_buffers=1)
        ca = _conv_a(proj, conv_a[l], seq=seq, d_conv=d_conv)
        attn = _diff_attention(proj, lam_q1[l], lam_k1[l], lam_q2[l], lam_k2[l], subln_g[l],
                               batch=batch, seq=seq, q_col=q_col, k_col=k_col, v_col=v_col,
                               lam_init=_lambda_init(l))
        merged = _merge(ca, attn, w_a_out, w_b_out, l, proj, b_gate[l],
                        gate_col=gate_col, d_model=d_model)
        mix = _matmul(merged, w_o, l, F32, tm=1024, tn=1024, name="w_o", w_buffers=1)
        xt, h2 = _resid_norm(xt, mix, norm_mix_post[l], norm_ffn_pre[l])

        up2 = _matmul(h2, w_ffn_in, l, BF16, tm=1024, tn=512, name="ffn_in")
        f = _ffn_gate(up2, conv_ffn[l], seq=seq, d_ff=d_ff)
        ffn = _matmul(f, w_ffn_out_bf, l, F32, tm=512, tn=512, name="ffn_out")
        g_next = norm_mix_pre[l + 1] if l + 1 < depth else None
        xt, h = _resid_norm(xt, ffn, norm_ffn_post[l], g_next)
    return xt.reshape(batch, seq, d_model)
```

```python
import functools
import math

import jax
import jax.numpy as jnp
from jax import lax
from jax.experimental import pallas as pl
from jax.experimental.pallas import tpu as pltpu

CHUNK = 64
CONV_WIDTH = 3
N_HEADS = 8
HEAD_DIM_QK = 128
HEAD_DIM_V = 256
EPS = 1e-6
NEG_INF = -1e30
LOG2E = math.log2(math.e)
QK_SCALE_LOG2 = HEAD_DIM_QK ** -0.5 * LOG2E

VMEM_LIMIT_BYTES = 56 * 1024 * 1024
HALO_ROWS = 16
F32_SUBLANES = 8

F32 = jnp.float32
BF16 = jnp.bfloat16


def _params(n_grid_dims):
    return pltpu.CompilerParams(
        dimension_semantics=("arbitrary",) * n_grid_dims,
        vmem_limit_bytes=VMEM_LIMIT_BYTES)


def _rms(x, g):
    return x * lax.rsqrt(jnp.mean(x * x, axis=-1, keepdims=True) + EPS) * g


def _norm_kernel(x_ref, g_ref, h_ref):
    h_ref[...] = _rms(x_ref[...], g_ref[...]).astype(h_ref.dtype)


def _norm(x, g, *, rows=256):
    t, d = x.shape
    return pl.pallas_call(
        _norm_kernel,
        grid=(t // rows,),
        in_specs=[pl.BlockSpec((rows, d), lambda i: (i, 0)),
                  pl.BlockSpec((1, d), lambda i: (0, 0))],
        out_specs=pl.BlockSpec((rows, d), lambda i: (i, 0)),
        out_shape=jax.ShapeDtypeStruct((t, d), BF16),
        compiler_params=_params(1),
        name="norm",
    )(x, g.reshape(1, d))


def _resid_norm_kernel(x_ref, y_ref, gpost_ref, gpre_ref, xo_ref, h_ref):
    x_new = x_ref[...] + _rms(y_ref[...], gpost_ref[...])
    xo_ref[...] = x_new
    h_ref[...] = _rms(x_new, gpre_ref[...]).astype(h_ref.dtype)


def _resid_kernel(x_ref, y_ref, gpost_ref, xo_ref):
    xo_ref[...] = x_ref[...] + _rms(y_ref[...], gpost_ref[...])


def _resid_norm(x, y, g_post, g_pre_next, *, rows=256):
    t, d = x.shape
    row_spec = pl.BlockSpec((rows, d), lambda i: (i, 0))
    vec_spec = pl.BlockSpec((1, d), lambda i: (0, 0))
    if g_pre_next is None:
        return pl.pallas_call(
            _resid_kernel,
            grid=(t // rows,),
            in_specs=[row_spec, row_spec, vec_spec],
            out_specs=row_spec,
            out_shape=jax.ShapeDtypeStruct((t, d), F32),
            compiler_params=_params(1),
            name="resid",
        )(x, y, g_post.reshape(1, d)), None
    return pl.pallas_call(
        _resid_norm_kernel,
        grid=(t // rows,),
        in_specs=[row_spec, row_spec, vec_spec, vec_spec],
        out_specs=[row_spec, row_spec],
        out_shape=[jax.ShapeDtypeStruct((t, d), F32), jax.ShapeDtypeStruct((t, d), BF16)],
        compiler_params=_params(1),
        name="resid_norm",
    )(x, y, g_post.reshape(1, d), g_pre_next.reshape(1, d))


def _cast_weight_once(w_ref, wbf_ref):
    @pl.when(pl.program_id(1) == 0)
    def _():
        wbf_ref[...] = w_ref[...].astype(wbf_ref.dtype)


def _mm_direct_kernel(a_ref, w_ref, o_ref):
    o_ref[...] = jnp.dot(a_ref[...], w_ref[...], preferred_element_type=F32).astype(o_ref.dtype)


def _mm_kernel(a_ref, w_ref, o_ref, wbf_ref):
    _cast_weight_once(w_ref, wbf_ref)
    o_ref[...] = jnp.dot(a_ref[...], wbf_ref[...], preferred_element_type=F32).astype(o_ref.dtype)


def _mm_colscale_kernel(a_ref, w_ref, s_ref, o_ref, wbf_ref):
    _cast_weight_once(w_ref, wbf_ref)
    acc = jnp.dot(a_ref[...], wbf_ref[...], preferred_element_type=F32)
    o_ref[...] = (acc * s_ref[...]).astype(o_ref.dtype)


def _matmul(a, w, layer, out_dtype, *, tm, tn, name, col_scale=None, w_buffers=2):
    t, k = a.shape
    n = w.shape[2]
    in_specs = [pl.BlockSpec((tm, k), lambda j, i: (i, 0)),
                pl.BlockSpec((None, k, tn), lambda j, i: (layer, 0, j),
                             pipeline_mode=pl.Buffered(w_buffers))]
    operands = [a, w]
    if col_scale is not None:
        in_specs.append(pl.BlockSpec((1, tn), lambda j, i: (0, j)))
        operands.append(col_scale.reshape(1, n))
    if w.dtype == BF16:
        assert col_scale is None
        body, scratch = _mm_direct_kernel, []
    else:
        body = _mm_kernel if col_scale is None else _mm_colscale_kernel
        scratch = [pltpu.VMEM((k, tn), BF16)]
    return pl.pallas_call(
        body,
        grid=(n // tn, t // tm),
        in_specs=in_specs,
        out_specs=pl.BlockSpec((tm, tn), lambda j, i: (i, j)),
        out_shape=jax.ShapeDtypeStruct((t, n), out_dtype),
        scratch_shapes=scratch,
        compiler_params=_params(2),
        name=name,
    )(*operands)


def _causal_conv3(u, halo, w_ref):
    rows, pad = u.shape[0], halo.shape[0]
    ext = jnp.concatenate([halo, u], axis=0)
    u1 = pltpu.roll(ext, 1, 0)[pad:pad + rows]
    u2 = pltpu.roll(ext, 2, 0)[pad:pad + rows]
    return w_ref[0:1, :] * u2 + w_ref[1:2, :] * u1 + w_ref[2:3, :] * u


def _halo_scale(rows, seq):
    at_start = (pl.program_id(0) * rows) % seq == 0
    return jnp.where(at_start, 0.0, 1.0).astype(F32)


def _conv_a_kernel(ain_ref, ab_ref, ac_ref, ain_h_ref, ac_h_ref, w_ref, o_ref, *, rows, seq):
    u = ac_ref[...].astype(F32) * ain_ref[...].astype(F32)
    halo = ac_h_ref[...].astype(F32) * ain_h_ref[...].astype(F32) * _halo_scale(rows, seq)
    y = ab_ref[...].astype(F32) * _causal_conv3(u, halo, w_ref)
    o_ref[...] = y.astype(o_ref.dtype)


def _conv_a(proj, conv_w, *, seq, d_conv, rows=512, cols=512):
    t = proj.shape[0]
    ncb = d_conv // cols
    hb = rows // HALO_ROWS

    def sect(s):
        return pl.BlockSpec((rows, cols), lambda i, j: (i, s * ncb + j))

    def halo(s):
        return pl.BlockSpec((HALO_ROWS, cols),
                            lambda i, j: (jnp.maximum(i * hb - 1, 0), s * ncb + j))

    return pl.pallas_call(
        functools.partial(_conv_a_kernel, rows=rows, seq=seq),
        grid=(t // rows, ncb),
        in_specs=[sect(0), sect(1), sect(2), halo(0), halo(2),
                  pl.BlockSpec((CONV_WIDTH, cols), lambda i, j: (0, j))],
        out_specs=pl.BlockSpec((rows, cols), lambda i, j: (i, j)),
        out_shape=jax.ShapeDtypeStruct((t, d_conv), BF16),
        compiler_params=_params(2),
        name="conv_a",
    )(proj, proj, proj, proj, proj, conv_w)


def _attn_kernel(slopes_ref, lq1_ref, lk1_ref, lq2_ref, lk2_ref, g_ref, q_ref, k_ref, v_ref,
                 o_ref, vt_ref, bias_ref, u_ref, bmax_ref, p_ref, alpha_ref, m_ref, l_ref, acc_ref,
                 *, lam_init, blk, seq):
    h = pl.program_id(1)
    qi = pl.program_id(2)
    slope2 = slopes_ref[h] * LOG2E

    @pl.when(qi == 0)
    def _per_head_setup():
        def xpose(c, carry):
            rows = pl.ds(pl.multiple_of(c * blk, blk), blk)
            vt_ref[:, rows] = v_ref[rows, :].T
            return carry
        lax.fori_loop(0, seq // blk, xpose, 0)
        krel = lax.broadcasted_iota(jnp.int32, (blk, blk), 0)
        qrel = lax.broadcasted_iota(jnp.int32, (blk, blk), 1)
        bias_ref[0] = slope2 * krel.astype(F32)
        visible = (krel // CHUNK) <= (qrel // CHUNK)
        bias_ref[1] = jnp.where(visible, slope2 * (qrel - jnp.abs(qrel - krel)).astype(F32), NEG_INF)

    m_ref[...] = jnp.full(m_ref.shape, NEG_INF, F32)
    l_ref[...] = jnp.zeros(l_ref.shape, F32)
    acc_ref[...] = jnp.zeros(acc_ref.shape, F32)

    n_blocks = qi + 1

    def rows_of(j):
        if isinstance(j, int):
            return pl.ds(j * blk, blk)
        return pl.ds(pl.multiple_of(j * blk, blk), blk)

    def scores(j, slot):
        k = k_ref[rows_of(j), :]
        which = (j == qi).astype(jnp.int32)
        for c in range(2):
            kc = k[:, c * HEAD_DIM_QK:(c + 1) * HEAD_DIM_QK]
            qc = q_ref[:, c * HEAD_DIM_QK:(c + 1) * HEAD_DIM_QK]
            u = lax.dot_general(kc, qc, (((1,), (1,)), ((), ())), preferred_element_type=F32)
            u = u + bias_ref[which]
            u_ref[slot, c] = u
            bmax_ref[slot, c] = jnp.max(u, axis=0, keepdims=True)

    def softmax(j, slot):
        off = slope2 * jnp.full((1, blk), (j - qi) * blk, jnp.int32).astype(F32)
        for c in range(2):
            m_old = m_ref[c]
            m_new = jnp.maximum(m_old, bmax_ref[slot, c] + off)
            alpha = jnp.exp2(m_old - m_new)
            p = jnp.exp2(u_ref[slot, c] - (m_new - off))
            l_ref[c] = alpha * l_ref[c] + jnp.sum(p, axis=0, keepdims=True)
            p_ref[slot, c] = p.astype(p_ref.dtype)
            alpha_ref[slot, c] = alpha
            m_ref[c] = m_new

    def values(j, slot):
        vt = vt_ref[:, rows_of(j)]
        for c in range(2):
            pv = jnp.dot(vt, p_ref[slot, c], preferred_element_type=F32)
            acc_ref[c] = alpha_ref[slot, c] * acc_ref[c] + pv

    def trip(t, slot):
        values(t, slot)
        softmax(t + 1, 1 - slot)
        scores(t + 2, slot)

    def trip_pair(i, carry):
        trip(2 * i, 0)
        trip(2 * i + 1, 1)
        return carry

    @pl.when(qi == 0)
    def _single_block():
        scores(0, 0)
        softmax(0, 0)
        values(0, 0)

    @pl.when(qi > 0)
    def _pipelined():
        scores(0, 0)
        scores(1, 1)
        softmax(0, 0)
        n_steady = n_blocks - 2
        lax.fori_loop(0, n_steady // 2, trip_pair, 0)

        @pl.when(n_steady % 2 == 1)
        def _():
            trip(n_steady - 1, 0)

        last_slot = 1 - n_blocks % 2
        values(n_blocks - 2, 1 - last_slot)
        softmax(n_blocks - 1, last_slot)
        values(n_blocks - 1, last_slot)

    lam =(jnp.exp(jnp.sum(lq1_ref[...] * lk1_ref[...], axis=-1, keepdims=True))
           - jnp.exp(jnp.sum(lq2_ref[...] * lk2_ref[...], axis=-1, keepdims=True))
           + lam_init)
    ot = acc_ref[0] * (1.0 / l_ref[0]) - acc_ref[1] * (lam / l_ref[1])
    ot = ot * lax.rsqrt(jnp.mean(ot * ot, axis=0, keepdims=True) + EPS)
    o_ref[...] = (ot.T * (g_ref[...] * (1.0 - lam_init))).astype(o_ref.dtype)


def _diff_attention(proj, lam_q1, lam_k1, lam_q2, lam_k2, subln_g, *, batch, seq, q_col, k_col,
                    v_col, lam_init, blk=512):
    t = proj.shape[0]
    hw = 2 * HEAD_DIM_QK
    assert hw == HEAD_DIM_V and seq % blk == 0 and blk % CHUNK == 0
    nq = seq // blk
    slopes = 2.0 ** (-8.0 * jnp.arange(1, N_HEADS + 1, dtype=F32) / N_HEADS)
    vec = lambda n: pl.BlockSpec((1, n), lambda b, h, i, s: (0, 0))
    grid_spec = pltpu.PrefetchScalarGridSpec(
        num_scalar_prefetch=1,
        grid=(batch, N_HEADS, nq),
        in_specs=[vec(HEAD_DIM_QK), vec(HEAD_DIM_QK), vec(HEAD_DIM_QK), vec(HEAD_DIM_QK),
                  vec(HEAD_DIM_V),
                  pl.BlockSpec((blk, hw), lambda b, h, i, s: (b * nq + i, q_col // hw + h)),
                  pl.BlockSpec((seq, hw), lambda b, h, i, s: (b, k_col // hw + h)),
                  pl.BlockSpec((seq, HEAD_DIM_V), lambda b, h, i, s: (b, v_col // HEAD_DIM_V + h))],
        out_specs=pl.BlockSpec((blk, HEAD_DIM_V), lambda b, h, i, s: (b * nq + i, h)),
        scratch_shapes=[pltpu.VMEM((HEAD_DIM_V, seq), BF16),
                        pltpu.VMEM((2, blk, blk), F32),
                        pltpu.VMEM((2, 2, blk, blk), F32),
                        pltpu.VMEM((2, 2, 1, blk), F32),
                        pltpu.VMEM((2, 2, blk, blk), BF16),
                        pltpu.VMEM((2, 2, 1, blk), F32),
                        pltpu.VMEM((2, 1, blk), F32), pltpu.VMEM((2, 1, blk), F32),
                        pltpu.VMEM((2, HEAD_DIM_V, blk), F32)],
    )
    return pl.pallas_call(
        functools.partial(_attn_kernel, lam_init=lam_init, blk=blk, seq=seq),
        grid_spec=grid_spec,
        out_shape=jax.ShapeDtypeStruct((t, N_HEADS * HEAD_DIM_V), BF16),
        compiler_params=_params(3),
        name="diff_attn",
    )(slopes, lam_q1.reshape(1, -1), lam_k1.reshape(1, -1), lam_q2.reshape(1, -1),
      lam_k2.reshape(1, -1), subln_g.reshape(1, -1), proj, proj, proj)


def _merge_kernel(ca_ref, at_ref, wa_ref, wb_ref, ga_ref, gb_ref, ba_ref, bb_ref, o_ref,
                  wabf_ref, wbbf_ref):
    _cast_weight_once(wa_ref, wabf_ref)
    _cast_weight_once(wb_ref, wbbf_ref)
    y_a = jnp.dot(ca_ref[...], wabf_ref[...], preferred_element_type=F32)
    y_b = jnp.dot(at_ref[...], wbbf_ref[...], preferred_element_type=F32)
    g_a = jax.nn.sigmoid(ga_ref[...].astype(F32) + ba_ref[...])
    g_b = jax.nn.sigmoid(gb_ref[...].astype(F32) + bb_ref[...])
    o_ref[...] = (g_a * y_a + g_b * y_b).astype(o_ref.dtype)


def _merge(ca, attn, w_a, w_b, layer, proj, b_gate, *, gate_col, d_model, tm=1024, tn=512):
    t, ka = ca.shape
    kb = attn.shape[1]
    gcb = gate_col // tn
    ncb = d_model // tn
    bias = b_gate.reshape(1, -1)
    return pl.pallas_call(
        _merge_kernel,
        grid=(ncb, t // tm),
        in_specs=[pl.BlockSpec((tm, ka), lambda j, i: (i, 0)),
                  pl.BlockSpec((tm, kb), lambda j, i: (i, 0)),
                  pl.BlockSpec((None, ka, tn), lambda j, i: (layer, 0, j)),
                  pl.BlockSpec((None, kb, tn), lambda j, i: (layer, 0, j)),
                  pl.BlockSpec((tm, tn), lambda j, i: (i, gcb + j)),
                  pl.BlockSpec((tm, tn), lambda j, i: (i, gcb + ncb + j)),
                  pl.BlockSpec((1, tn), lambda j, i: (0, j)),
                  pl.BlockSpec((1, tn), lambda j, i: (0, ncb + j))],
        out_specs=pl.BlockSpec((tm, tn), lambda j, i: (i, j)),
        out_shape=jax.ShapeDtypeStruct((t, d_model), BF16),
        scratch_shapes=[pltpu.VMEM((ka, tn), BF16), pltpu.VMEM((kb, tn), BF16)],
        compiler_params=_params(2),
        name="merge",
    )(ca, attn, w_a, w_b, proj, proj, bias, bias)


def _ffn_in_kernel(a_ref, wg_ref, wu_ref, cw_ref, o_ref, wgbf_ref, wubf_ref, halo_ref,
                   *, tiles_per_seq):
    _cast_weight_once(wg_ref, wgbf_ref)
    _cast_weight_once(wu_ref, wubf_ref)

    @pl.when(pl.program_id(1) % tiles_per_seq == 0)
    def _sequence_start():
        halo_ref[...] = jnp.zeros(halo_ref.shape, F32)

    a = a_ref[...]
    gate = jnp.dot(a, wgbf_ref[...], preferred_element_type=F32)
    z = _causal_conv3(gate, halo_ref[...], cw_ref)
    halo_ref[...] = gate[gate.shape[0] - halo_ref.shape[0]:, :]
    c0 = math.sqrt(2.0 / math.pi)
    act = 0.5 * z * (1.0 + jnp.tanh(c0 * (z + 0.044715 * (z * z * z))))
    up = jnp.dot(a, wubf_ref[...], preferred_element_type=F32)
    o_ref[...] = (act * up).astype(o_ref.dtype)


def _ffn_in(h, w, layer, conv_w, *, seq, d_ff, tm=1024, tn=256):
    t, k = h.shape
    ncb = d_ff // tn
    assert seq % tm == 0
    return pl.pallas_call(
        functools.partial(_ffn_in_kernel, tiles_per_seq=seq // tm),
        grid=(ncb, t // tm),
        in_specs=[pl.BlockSpec((tm, k), lambda j, i: (i, 0)),
                  pl.BlockSpec((None, k, tn), lambda j, i: (layer, 0, j)),
                  pl.BlockSpec((None, k, tn), lambda j, i: (layer, 0, ncb + j)),
                  pl.BlockSpec((CONV_WIDTH, tn), lambda j, i: (0, j))],
        out_specs=pl.BlockSpec((tm, tn), lambda j, i: (i, j)),
        out_shape=jax.ShapeDtypeStruct((t, d_ff), BF16),
        scratch_shapes=[pltpu.VMEM((k, tn), BF16), pltpu.VMEM((k, tn), BF16),
                        pltpu.VMEM((F32_SUBLANES, tn), F32)],
        compiler_params=_params(2),
        name="ffn_in",
    )(h, w, w, conv_w)


def _lambda_init(layer_idx):
    return 0.8 - 0.6 * math.exp(-0.3 * layer_idx)


def kernel(x, w_in, b_gate, conv_a, w_a_out, lam_q1, lam_k1, lam_q2, lam_k2, subln_g, w_b_out, w_o,
           norm_mix_pre, norm_mix_post, w_ffn_in, conv_ffn, w_ffn_out, norm_ffn_pre, norm_ffn_post):
    batch, seq, d_model = x.shape
    depth = w_in.shape[0]
    d_conv = conv_a.shape[-1]
    d_ff = conv_ffn.shape[-1]
    d_qk = N_HEADS * 2 * HEAD_DIM_QK
    q_col = 3 * d_conv
    k_col = q_col + d_qk
    v_col = k_col + d_qk
    gate_col = v_col + N_HEADS * HEAD_DIM_V

    d_in = w_in.shape[-1]
    q_scale = jnp.ones((d_in,), F32).at[q_col:k_col].set(QK_SCALE_LOG2)

    w_ffn_out_bf = w_ffn_out.astype(BF16)

    xt = x.reshape(batch * seq, d_model)
    h = _norm(xt, norm_mix_pre[0])
    for l in range(depth):
        proj = _matmul(h, w_in, l, BF16, tm=1024, tn=1024, name="in_proj", col_scale=q_scale,
                       w_buffers=1)
        ca = _conv_a(proj, conv_a[l], seq=seq, d_conv=d_conv)
        attn = _diff_attention(proj, lam_q1[l], lam_k1[l], lam_q2[l], lam_k2[l], subln_g[l],
                               batch=batch, seq=seq, q_col=q_col, k_col=k_col, v_col=v_col,
                               lam_init=_lambda_init(l))
        merged = _merge(ca, attn, w_a_out, w_b_out, l, proj, b_gate[l],
                        gate_col=gate_col, d_model=d_model)
        mix = _matmul(merged, w_o, l, F32, tm=1024, tn=1024, name="w_o", w_buffers=1)
        xt, h2 = _resid_norm(xt, mix, norm_mix_post[l], norm_ffn_pre[l])

        f = _ffn_in(h2, w_ffn_in, l, conv_ffn[l], seq=seq, d_ff=d_ff)
        ffn = _matmul(f, w_ffn_out_bf, l, F32, tm=512, tn=512, name="ffn_out")
        g_next = norm_mix_pre[l + 1] if l + 1 < depth else None
        xt, h = _resid_norm(xt, ffn, norm_ffn_post[l], g_next)
    return xt.reshape(batch, seq, d_model)
```

```python
import functools
import math

import jax
import jax.numpy as jnp
from jax import lax
from jax.experimental import pallas as pl
from jax.experimental.pallas import tpu as pltpu

CHUNK = 64
CONV_WIDTH = 3
N_HEADS = 8
HEAD_DIM_QK = 128
HEAD_DIM_V = 256
EPS = 1e-6
NEG_INF = -1e30
LOG2E = math.log2(math.e)
QK_SCALE_LOG2 = HEAD_DIM_QK ** -0.5 * LOG2E

VMEM_LIMIT_BYTES = 56 * 1024 * 1024
HALO_ROWS = 16
F32_SUBLANES = 8
W_CAST_CHUNKS = 4

F32 = jnp.float32
BF16 = jnp.bfloat16


def _params(n_grid_dims):
    return pltpu.CompilerParams(
        dimension_semantics=("arbitrary",) * n_grid_dims,
        vmem_limit_bytes=VMEM_LIMIT_BYTES)


def _rms(x, g):
    return x * lax.rsqrt(jnp.mean(x * x, axis=-1, keepdims=True) + EPS) * g


def _norm_kernel(x_ref, g_ref, h_ref):
    h_ref[...] = _rms(x_ref[...], g_ref[...]).astype(h_ref.dtype)


def _norm(x, g, *, rows=256):
    t, d = x.shape
    return pl.pallas_call(
        _norm_kernel,
        grid=(t // rows,),
        in_specs=[pl.BlockSpec((rows, d), lambda i: (i, 0)),
                  pl.BlockSpec((1, d), lambda i: (0, 0))],
        out_specs=pl.BlockSpec((rows, d), lambda i: (i, 0)),
        out_shape=jax.ShapeDtypeStruct((t, d), BF16),
        compiler_params=_params(1),
        name="norm",
    )(x, g.reshape(1, d))


def _resid_norm_kernel(x_ref, y_ref, gpost_ref, gpre_ref, xo_ref, h_ref):
    x_new = x_ref[...] + _rms(y_ref[...], gpost_ref[...])
    xo_ref[...] = x_new
    h_ref[...] = _rms(x_new, gpre_ref[...]).astype(h_ref.dtype)


def _resid_kernel(x_ref, y_ref, gpost_ref, xo_ref):
    xo_ref[...] = x_ref[...] + _rms(y_ref[...], gpost_ref[...])


def _resid_norm(x, y, g_post, g_pre_next, *, rows=256):
    t, d = x.shape
    row_spec = pl.BlockSpec((rows, d), lambda i: (i, 0))
    vec_spec = pl.BlockSpec((1, d), lambda i: (0, 0))
    if g_pre_next is None:
        return pl.pallas_call(
            _resid_kernel,
            grid=(t // rows,),
            in_specs=[row_spec, row_spec, vec_spec],
            out_specs=row_spec,
            out_shape=jax.ShapeDtypeStruct((t, d), F32),
            compiler_params=_params(1),
            name="resid",
        )(x, y, g_post.reshape(1, d)), None
    return pl.pallas_call(
        _resid_norm_kernel,
        grid=(t // rows,),
        in_specs=[row_spec, row_spec, vec_spec, vec_spec],
        out_specs=[row_spec, row_spec],
        out_shape=[jax.ShapeDtypeStruct((t, d), F32), jax.ShapeDtypeStruct((t, d), BF16)],
        compiler_params=_params(1),
        name="resid_norm",
    )(x, y, g_post.reshape(1, d), g_pre_next.reshape(1, d))


def _dot_rounding_in_chunks(a_ref, w_ref, wbf_ref):
    kc = w_ref.shape[0] // W_CAST_CHUNKS
    acc = None
    for c in range(W_CAST_CHUNKS):
        ks = pl.ds(c * kc, kc)
        wbf_ref[ks, :] = w_ref[ks, :].astype(wbf_ref.dtype)
        part = jnp.dot(a_ref[:, ks], wbf_ref[ks, :], preferred_element_type=F32)
        acc = part if acc is None else acc + part
    return acc


def _with_rounded_weights(operands, emit):
    @pl.when(pl.program_id(1) == 0)
    def _first_token_tile():
        emit(*[_dot_rounding_in_chunks(*op) for op in operands])

    @pl.when(pl.program_id(1) != 0)
    def _later_token_tiles():
        emit(*[jnp.dot(a_ref[...], wbf_ref[...], preferred_element_type=F32)
               for a_ref, _, wbf_ref in operands])


def _mm_direct_kernel(a_ref, w_ref, o_ref):
    o_ref[...] = jnp.dot(a_ref[...], w_ref[...], preferred_element_type=F32).astype(o_ref.dtype)


def _mm_kernel(a_ref, w_ref, o_ref, wbf_ref):
    def emit(acc):
        o_ref[...] = acc.astype(o_ref.dtype)
    _with_rounded_weights([(a_ref, w_ref, wbf_ref)], emit)


def _mm_colscale_kernel(a_ref, w_ref, s_ref, o_ref, wbf_ref):
    def emit(acc):
        o_ref[...] = (acc * s_ref[...]).astype(o_ref.dtype)
    _with_rounded_weights([(a_ref, w_ref, wbf_ref)], emit)


def _matmul(a, w, layer, out_dtype, *, tm, tn, name, col_scale=None, w_buffers=2):
    t, k = a.shape
    n = w.shape[2]
    in_specs = [pl.BlockSpec((tm, k), lambda j, i: (i, 0)),
                pl.BlockSpec((None, k, tn), lambda j, i: (layer, 0, j),
                             pipeline_mode=pl.Buffered(w_buffers))]
    operands = [a, w]
    if col_scale is not None:
        in_specs.append(pl.BlockSpec((1, tn), lambda j, i: (0, j)))
        operands.append(col_scale.reshape(1, n))
    if w.dtype == BF16:
        assert col_scale is None
        body, scratch = _mm_direct_kernel, []
    else:
        body = _mm_kernel if col_scale is None else _mm_colscale_kernel
        scratch = [pltpu.VMEM((k, tn), BF16)]
    return pl.pallas_call(
        body,
        grid=(n // tn, t // tm),
        in_specs=in_specs,
        out_specs=pl.BlockSpec((tm, tn), lambda j, i: (i, j)),
        out_shape=jax.ShapeDtypeStruct((t, n), out_dtype),
        scratch_shapes=scratch,
        compiler_params=_params(2),
        name=name,
    )(*operands)


def _causal_conv3(u, halo, w_ref):
    rows, pad = u.shape[0], halo.shape[0]
    ext = jnp.concatenate([halo, u], axis=0)
    u1 = pltpu.roll(ext, 1, 0)[pad:pad + rows]
    u2 = pltpu.roll(ext, 2, 0)[pad:pad + rows]
    return w_ref[0:1, :] * u2 + w_ref[1:2, :] * u1 + w_ref[2:3, :] * u


def _halo_scale(rows, seq):
    at_start = (pl.program_id(0) * rows) % seq == 0
    return jnp.where(at_start, 0.0, 1.0).astype(F32)


def _conv_a_kernel(ain_ref, ab_ref, ac_ref, ain_h_ref, ac_h_ref, w_ref, o_ref, *, rows, seq):
    u = ac_ref[...].astype(F32) * ain_ref[...].astype(F32)
    halo = ac_h_ref[...].astype(F32) * ain_h_ref[...].astype(F32) * _halo_scale(rows, seq)
    y = ab_ref[...].astype(F32) * _causal_conv3(u, halo, w_ref)
    o_ref[...] = y.astype(o_ref.dtype)


def _conv_a(proj, conv_w, *, seq, d_conv, rows=512, cols=512):
    t = proj.shape[0]
    ncb = d_conv // cols
    hb = rows // HALO_ROWS

    def sect(s):
        return pl.BlockSpec((rows, cols), lambda i, j: (i, s * ncb + j))

    def halo(s):
        return pl.BlockSpec((HALO_ROWS, cols),
                            lambda i, j: (jnp.maximum(i * hb - 1, 0), s * ncb + j))

    return pl.pallas_call(
        functools.partial(_conv_a_kernel, rows=rows, seq=seq),
        grid=(t // rows, ncb),
        in_specs=[sect(0), sect(1), sect(2), halo(0), halo(2),
                  pl.BlockSpec((CONV_WIDTH, cols), lambda i, j: (0, j))],
        out_specs=pl.BlockSpec((rows, cols), lambda i, j: (i, j)),
        out_shape=jax.ShapeDtypeStruct((t, d_conv), BF16),
        compiler_params=_params(2),
        name="conv_a",
    )(proj, proj, proj, proj, proj, conv_w)


def _attn_kernel(slopes_ref, lq1_ref, lk1_ref, lq2_ref, lk2_ref, g_ref, q_ref, k_ref, v_ref,
                 o_ref, vt_ref, bias_ref, u_ref, p_ref, alpha_ref, m_ref, acc_ref,
                 *, lam_init, blk, seq):
    h = pl.program_id(1)
    slope2 = slopes_ref[h] * LOG2E
    nq = seq // blk
    n_items = nq * (nq + 1) // 2
    assert nq >= 2 and n_items % 2 == 0

    def xpose(c, carry):
        rows = pl.ds(pl.multiple_of(c * blk, blk), blk)
        vt_ref[0:HEAD_DIM_V, rows] = v_ref[rows, :].T
        return carry
    lax.fori_loop(0, nq, xpose, 0)
    pad_row = lax.broadcasted_iota(jnp.int32, (HALO_ROWS, seq), 0)
    vt_ref[HEAD_DIM_V:, :] = jnp.where(pad_row == 0, 1.0, 0.0).astype(vt_ref.dtype)
    krel = lax.broadcasted_iota(jnp.int32, (blk, blk), 0)
    qrel = lax.broadcasted_iota(jnp.int32, (blk, blk), 1)
    bias_ref[0] = slope2 * krel.astype(F32)
    visible = (krel // CHUNK) <= (qrel // CHUNK)
    bias_ref[1] = jnp.where(visible, slope2 * (qrel - jnp.abs(qrel - krel)).astype(F32), NEG_INF)

    lam = (jnp.exp(jnp.sum(lq1_ref[...] * lk1_ref[...], axis=-1, keepdims=True))
           - jnp.exp(jnp.sum(lq2_ref[...] * lk2_ref[...], axis=-1, keepdims=True))
           + lam_init)
    out_gain = g_ref[...] * (1.0 - lam_init)

    def rows_of(j):
        if isinstance(j, int):
            return pl.ds(j * blk, blk)
        return pl.ds(pl.multiple_of(j * blk, blk), blk)

    def next_item(item):
        qi, j = item
        last = j == qi
        return jnp.where(last, qi + 1, qi), jnp.where(last, 0, j + 1)

    def scores(item, slot):
        qi, j = item
        k = k_ref[rows_of(j), :]
        q = q_ref[rows_of(qi), :]
        which = jnp.asarray(j == qi, jnp.int32)
        for c in range(2):
            kc = k[:, c * HEAD_DIM_QK:(c + 1) * HEAD_DIM_QK]
            qc = q[:, c * HEAD_DIM_QK:(c + 1) * HEAD_DIM_QK]
            u = lax.dot_general(kc, qc, (((1,), (1,)), ((), ())), preferred_element_type=F32)
            u_ref[slot, c] = u + bias_ref[which]

    def softmax(item, slot):
        qi, j = item
        off = slope2 * jnp.full((1, blk), (j - qi) * blk, jnp.int32).astype(F32)
        first = jnp.full((1, blk), j, jnp.int32) == 0
        for c in range(2):
            m_old = jnp.where(first, NEG_INF, m_ref[c])
            m_new = jnp.maximum(m_old, jnp.max(u_ref[slot, c], axis=0, keepdims=True) + off)
            p_ref[slot, c] = jnp.exp2(u_ref[slot, c] - (m_new - off)).astype(p_ref.dtype)
            alpha_ref[slot, c] = jnp.exp2(m_old - m_new)
            m_ref[c] = m_new

    def values(item, slot):
        qi, j = item
        vt = vt_ref[:, rows_of(j)]
        for c in range(2):
            pv = jnp.dot(vt, p_ref[slot, c], preferred_element_type=F32)
            acc_ref[c] = alpha_ref[slot, c] * acc_ref[c] + pv

    def finish_query_block(qi):
        l0 = acc_ref[0, HEAD_DIM_V:HEAD_DIM_V + 1, :]
        l1 = acc_ref[1, HEAD_DIM_V:HEAD_DIM_V + 1, :]
        ot = (acc_ref[0, 0:HEAD_DIM_V, :] * (1.0 / l0)
              - acc_ref[1, 0:HEAD_DIM_V, :] * (lam / l1))
        ot = ot * lax.rsqrt(jnp.mean(ot * ot, axis=0, keepdims=True) + EPS)
        o_ref[rows_of(qi), :] = (ot.T * out_gain).astype(o_ref.dtype)

    def trip(items, slot):
        done, mid, ahead = items
        values(done, slot)
        softmax(mid, 1 - slot)
        scores(ahead, slot)

        @pl.when(done[1] == done[0])
        def _():
            finish_query_block(done[0])
        return mid, ahead, next_item(ahead)

    def trip_pair(i, items):
        return trip(trip(items, 0), 1)

    acc_ref[...] = jnp.zeros(acc_ref.shape, F32)
    m_ref[...] = jnp.full(m_ref.shape, NEG_INF, F32)
    zero = jnp.int32(0)
    item0, item1 = (zero, zero), (zero + 1, zero)
    scores(item0, 0)
    scores(item1, 1)
    softmax(item0, 0)
    lax.fori_loop(0, (n_items - 2) // 2, trip_pair, (item0, item1, next_item(item1)))
    values((nq - 1, nq - 2), 0)
    softmax((nq - 1, nq - 1), 1)
    values((nq - 1, nq - 1), 1)
    finish_query_block(nq - 1)


def _diff_attention(proj, lam_q1, lam_k1, lam_q2, lam_k2, subln_g, *, batch, seq, q_col, k_col,
                    v_col, lam_init, blk=512):
    t = proj.shape[0]
    hw = 2 * HEAD_DIM_QK
    assert hw == HEAD_DIM_V and seq % blk == 0 and blk % CHUNK == 0
    vt_rows = HEAD_DIM_V + HALO_ROWS
    slopes = 2.0 ** (-8.0 * jnp.arange(1, N_HEADS + 1, dtype=F32) / N_HEADS)
    vec = lambda n: pl.BlockSpec((1, n), lambda b, h, s: (0, 0))
    head_cols = lambda col: pl.BlockSpec((seq, hw), lambda b, h, s: (b, col // hw + h))
    grid_spec = pltpu.PrefetchScalarGridSpec(
        num_scalar_prefetch=1,
        grid=(batch, N_HEADS),
        in_specs=[vec(HEAD_DIM_QK), vec(HEAD_DIM_QK), vec(HEAD_DIM_QK), vec(HEAD_DIM_QK),
                  vec(HEAD_DIM_V), head_cols(q_col), head_cols(k_col), head_cols(v_col)],
        out_specs=pl.BlockSpec((seq, HEAD_DIM_V), lambda b, h, s: (b, h)),
        scratch_shapes=[pltpu.VMEM((vt_rows, seq), BF16),
                        pltpu.VMEM((2, blk, blk), F32),
                        pltpu.VMEM((2, 2, blk, blk), F32),
                        pltpu.VMEM((2, 2, blk, blk), BF16),
                        pltpu.VMEM((2, 2, 1, blk), F32),
                        pltpu.VMEM((2, 1, blk), F32),
                        pltpu.VMEM((2, vt_rows, blk), F32)],
    )
    return pl.pallas_call(
        functools.partial(_attn_kernel, lam_init=lam_init, blk=blk, seq=seq),
        grid_spec=grid_spec,
        out_shape=jax.ShapeDtypeStruct((t, N_HEADS * HEAD_DIM_V), BF16),
        compiler_params=_params(2),
        name="diff_attn",
    )(slopes, lam_q1.reshape(1, -1), lam_k1.reshape(1, -1), lam_q2.reshape(1, -1),
      lam_k2.reshape(1, -1), subln_g.reshape(1, -1), proj, proj, proj)


def _merge_kernel(ca_ref, at_ref, wa_ref, wb_ref, ga_ref, gb_ref, ba_ref, bb_ref, o_ref,
                  wabf_ref, wbbf_ref):
    def emit(y_a, y_b):
        g_a = jax.nn.sigmoid(ga_ref[...].astype(F32) + ba_ref[...])
        g_b = jax.nn.sigmoid(gb_ref[...].astype(F32) + bb_ref[...])
        o_ref[...] = (g_a * y_a + g_b * y_b).astype(o_ref.dtype)
    _with_rounded_weights([(ca_ref, wa_ref, wabf_ref), (at_ref, wb_ref, wbbf_ref)], emit)


def _merge(ca, attn, w_a, w_b, layer, proj, b_gate, *, gate_col, d_model, tm=1024, tn=512):
    t, ka = ca.shape
    kb = attn.shape[1]
    gcb = gate_col // tn
    ncb = d_model // tn
    bias = b_gate.reshape(1, -1)
    return pl.pallas_call(
        _merge_kernel,
        grid=(ncb, t // tm),
        in_specs=[pl.BlockSpec((tm, ka), lambda j, i: (i, 0)),
                  pl.BlockSpec((tm, kb), lambda j, i: (i, 0)),
                  pl.BlockSpec((None, ka, tn), lambda j, i: (layer, 0, j)),
                  pl.BlockSpec((None, kb, tn), lambda j, i: (layer, 0, j)),
                  pl.BlockSpec((tm, tn), lambda j, i: (i, gcb + j)),
                  pl.BlockSpec((tm, tn), lambda j, i: (i, gcb + ncb + j)),
                  pl.BlockSpec((1, tn), lambda j, i: (0, j)),
                  pl.BlockSpec((1, tn), lambda j, i: (0, ncb + j))],
        out_specs=pl.BlockSpec((tm, tn), lambda j, i: (i, j)),
        out_shape=jax.ShapeDtypeStruct((t, d_model), BF16),
        scratch_shapes=[pltpu.VMEM((ka, tn), BF16), pltpu.VMEM((kb, tn), BF16)],
        compiler_params=_params(2),
        name="merge",
    )(ca, attn, w_a, w_b, proj, proj, bias, bias)


def _ffn_in_kernel(a_ref, wg_ref, wu_ref, cw_ref, o_ref, wgbf_ref, wubf_ref, halo_ref,
                   *, tiles_per_seq):
    @pl.when(pl.program_id(1) % tiles_per_seq == 0)
    def _sequence_start():
        halo_ref[...] = jnp.zeros(halo_ref.shape, F32)

    def emit(gate, up):
        z = _causal_conv3(gate, halo_ref[...], cw_ref)
        halo_ref[...] = gate[gate.shape[0] - halo_ref.shape[0]:, :]
        c0 = math.sqrt(2.0 / math.pi)
        act = 0.5 * z * (1.0 + jnp.tanh(c0 * (z + 0.044715 * (z * z * z))))
        o_ref[...] = (act * up).astype(o_ref.dtype)
    _with_rounded_weights([(a_ref, wg_ref, wgbf_ref), (a_ref, wu_ref, wubf_ref)], emit)


def _ffn_in(h, w, layer, conv_w, *, seq, d_ff, tm=1024, tn=256):
    t, k = h.shape
    ncb = d_ff // tn
    assert seq % tm == 0
    return pl.pallas_call(
        functools.partial(_ffn_in_kernel, tiles_per_seq=seq // tm),
        grid=(ncb, t // tm),
        in_specs=[pl.BlockSpec((tm, k), lambda j, i: (i, 0)),
                  pl.BlockSpec((None, k, tn), lambda j, i: (layer, 0, j)),
                  pl.BlockSpec((None, k, tn), lambda j, i: (layer, 0, ncb + j)),
                  pl.BlockSpec((CONV_WIDTH, tn), lambda j, i: (0, j))],
        out_specs=pl.BlockSpec((tm, tn), lambda j, i: (i, j)),
        out_shape=jax.ShapeDtypeStruct((t, d_ff), BF16),
        scratch_shapes=[pltpu.VMEM((k, tn), BF16), pltpu.VMEM((k, tn), BF16),
                        pltpu.VMEM((F32_SUBLANES, tn), F32)],
        compiler_params=_params(2),
        name="ffn_in",
    )(h, w, w, conv_w)


def _lambda_init(layer_idx):
    return 0.8 - 0.6 * math.exp(-0.3 * layer_idx)


def kernel(x, w_in, b_gate, conv_a, w_a_out, lam_q1, lam_k1, lam_q2, lam_k2, subln_g, w_b_out, w_o,
           norm_mix_pre, norm_mix_post, w_ffn_in, conv_ffn, w_ffn_out, norm_ffn_pre, norm_ffn_post):
    batch, seq, d_model = x.shape
    depth = w_in.shape[0]
    d_conv = conv_a.shape[-1]
    d_ff = conv_ffn.shape[-1]
    d_qk = N_HEADS * 2 * HEAD_DIM_QK
    q_col = 3 * d_conv
    k_col = q_col + d_qk
    v_col = k_col + d_qk
    gate_col = v_col + N_HEADS * HEAD_DIM_V

    d_in = w_in.shape[-1]
    q_scale = jnp.ones((d_in,), F32).at[q_col:k_col].set(QK_SCALE_LOG2)

    w_ffn_out_bf = w_ffn_out.astype(BF16)

    xt = x.reshape(batch * seq, d_model)
    h = _norm(xt, norm_mix_pre[0])
    for l in range(depth):
        proj = _matmul(h, w_in, l, BF16, tm=1024, tn=1024, name="in_proj", col_scale=q_scale,
                       w_buffers=1)
        ca = _conv_a(proj, conv_a[l], seq=seq, d_conv=d_conv)
        attn = _diff_attention(proj, lam_q1[l], lam_k1[l], lam_q2[l], lam_k2[l], subln_g[l],
                               batch=batch, seq=seq, q_col=q_col, k_col=k_col, v_col=v_col,
                               lam_init=_lambda_init(l))
        merged = _merge(ca, attn, w_a_out, w_b_out, l, proj, b_gate[l],
                        gate_col=gate_col, d_model=d_model)
        mix = _matmul(merged, w_o, l, F32, tm=1024, tn=1024, name="w_o", w_buffers=1)
        xt, h2 = _resid_norm(xt, mix, norm_mix_post[l], norm_ffn_pre[l])

        f = _ffn_in(h2, w_ffn_in, l, conv_ffn[l], seq=seq, d_ff=d_ff)
        ffn = _matmul(f, w_ffn_out_bf, l, F32, tm=512, tn=512, name="ffn_out")
        g_next = norm_mix_pre[l + 1] if l + 1 < depth else None
        xt, h = _resid_norm(xt, ffn, norm_ffn_post[l], g_next)
    return xt.reshape(batch, seq, d_model)
```

```python
import functools
import math

import jax
import jax.numpy as jnp
from jax import lax
from jax.experimental import pallas as pl
from jax.experimental.pallas import tpu as pltpu

CHUNK = 64
CONV_WIDTH = 3
N_HEADS = 8
HEAD_DIM_QK = 128
HEAD_DIM_V = 256
EPS = 1e-6
NEG_INF = -1e30
UNDERFLOW_LOG2 = 150.0
LOG2E = math.log2(math.e)
QK_SCALE_LOG2 = HEAD_DIM_QK ** -0.5 * LOG2E

VMEM_LIMIT_BYTES = 56 * 1024 * 1024
HALO_ROWS = 16
F32_SUBLANES = 8
W_CAST_CHUNKS = 4

F32 = jnp.float32
BF16 = jnp.bfloat16


def _params(n_grid_dims):
    return pltpu.CompilerParams(
        dimension_semantics=("arbitrary",) * n_grid_dims,
        vmem_limit_bytes=VMEM_LIMIT_BYTES)


def _rms(x, g):
    return x * lax.rsqrt(jnp.mean(x * x, axis=-1, keepdims=True) + EPS) * g


def _norm_kernel(x_ref, g_ref, h_ref):
    h_ref[...] = _rms(x_ref[...], g_ref[...]).astype(h_ref.dtype)


def _norm(x, g, *, rows=256):
    t, d = x.shape
    return pl.pallas_call(
        _norm_kernel,
        grid=(t // rows,),
        in_specs=[pl.BlockSpec((rows, d), lambda i: (i, 0)),
                  pl.BlockSpec((1, d), lambda i: (0, 0))],
        out_specs=pl.BlockSpec((rows, d), lambda i: (i, 0)),
        out_shape=jax.ShapeDtypeStruct((t, d), BF16),
        compiler_params=_params(1),
        name="norm",
    )(x, g.reshape(1, d))


def _resid_norm_kernel(x_ref, y_ref, gpost_ref, gpre_ref, xo_ref, h_ref):
    x_new = x_ref[...] + _rms(y_ref[...].astype(F32), gpost_ref[...])
    xo_ref[...] = x_new
    h_ref[...] = _rms(x_new, gpre_ref[...]).astype(h_ref.dtype)


def _resid_kernel(x_ref, y_ref, gpost_ref, xo_ref):
    xo_ref[...] = x_ref[...] + _rms(y_ref[...].astype(F32), gpost_ref[...])


def _resid_norm(x, y, g_post, g_pre_next, *, rows=256):
    t, d = x.shape
    row_spec = pl.BlockSpec((rows, d), lambda i: (i, 0))
    vec_spec = pl.BlockSpec((1, d), lambda i: (0, 0))
    if g_pre_next is None:
        return pl.pallas_call(
            _resid_kernel,
            grid=(t // rows,),
            in_specs=[row_spec, row_spec, vec_spec],
            out_specs=row_spec,
            out_shape=jax.ShapeDtypeStruct((t, d), F32),
            compiler_params=_params(1),
            name="resid",
        )(x, y, g_post.reshape(1, d)), None
    return pl.pallas_call(
        _resid_norm_kernel,
        grid=(t // rows,),
        in_specs=[row_spec, row_spec, vec_spec, vec_spec],
        out_specs=[row_spec, row_spec],
        out_shape=[jax.ShapeDtypeStruct((t, d), F32), jax.ShapeDtypeStruct((t, d), BF16)],
        compiler_params=_params(1),
        name="resid_norm",
    )(x, y, g_post.reshape(1, d), g_pre_next.reshape(1, d))


def _dot_rounding_in_chunks(a_ref, w_ref, wbf_ref):
    kc = w_ref.shape[0] // W_CAST_CHUNKS
    acc = None
    for c in range(W_CAST_CHUNKS):
        ks = pl.ds(c * kc, kc)
        wbf_ref[ks, :] = w_ref[ks, :].astype(wbf_ref.dtype)
        part = jnp.dot(a_ref[:, ks], wbf_ref[ks, :], preferred_element_type=F32)
        acc = part if acc is None else acc + part
    return acc


def _with_rounded_weights(operands, emit):
    @pl.when(pl.program_id(1) == 0)
    def _first_token_tile():
        emit(*[_dot_rounding_in_chunks(*op) for op in operands])

    @pl.when(pl.program_id(1) != 0)
    def _later_token_tiles():
        emit(*[jnp.dot(a_ref[...], wbf_ref[...], preferred_element_type=F32)
               for a_ref, _, wbf_ref in operands])


def _mm_direct_kernel(a_ref, w_ref, o_ref):
    o_ref[...] = jnp.dot(a_ref[...], w_ref[...], preferred_element_type=F32).astype(o_ref.dtype)


def _mm_kernel(a_ref, w_ref, o_ref, wbf_ref):
    def emit(acc):
        o_ref[...] = acc.astype(o_ref.dtype)
    _with_rounded_weights([(a_ref, w_ref, wbf_ref)], emit)


def _mm_colscale_kernel(a_ref, w_ref, s_ref, o_ref, wbf_ref):
    def emit(acc):
        o_ref[...] = (acc * s_ref[...]).astype(o_ref.dtype)
    _with_rounded_weights([(a_ref, w_ref, wbf_ref)], emit)


def _matmul(a, w, layer, out_dtype, *, tm, tn, name, col_scale=None, w_buffers=2, col_start=0):
    t, k = a.shape
    n = w.shape[2] - col_start
    c0 = col_start // tn
    assert col_start % tn == 0
    in_specs = [pl.BlockSpec((tm, k), lambda j, i: (i, 0)),
                pl.BlockSpec((None, k, tn), lambda j, i: (layer, 0, c0 + j),
                             pipeline_mode=pl.Buffered(w_buffers))]
    operands = [a, w]
    if col_scale is not None:
        in_specs.append(pl.BlockSpec((1, tn), lambda j, i: (0, c0 + j)))
        operands.append(col_scale.reshape(1, -1))
    if w.dtype == BF16:
        assert col_scale is None
        body, scratch = _mm_direct_kernel, []
    else:
        body = _mm_kernel if col_scale is None else _mm_colscale_kernel
        scratch = [pltpu.VMEM((k, tn), BF16)]
    return pl.pallas_call(
        body,
        grid=(n // tn, t // tm),
        in_specs=in_specs,
        out_specs=pl.BlockSpec((tm, tn), lambda j, i: (i, j)),
        out_shape=jax.ShapeDtypeStruct((t, n), out_dtype),
        scratch_shapes=scratch,
        compiler_params=_params(2),
        name=name,
    )(*operands)


def _causal_conv3(u, halo, w_ref):
    rows, pad = u.shape[0], halo.shape[0]
    ext = jnp.concatenate([halo, u], axis=0)
    u1 = pltpu.roll(ext, 1, 0)[pad:pad + rows]
    u2 = pltpu.roll(ext, 2, 0)[pad:pad + rows]
    return w_ref[0:1, :] * u2 + w_ref[1:2, :] * u1 + w_ref[2:3, :] * u


def _conv_a_kernel(a_ref, win_ref, wb_ref, wc_ref, cw_ref, o_ref, winbf_ref, wbbf_ref, wcbf_ref,
                   halo_ref, *, tiles_per_seq):
    @pl.when(pl.program_id(1) % tiles_per_seq == 0)
    def _sequence_start():
        halo_ref[...] = jnp.zeros(halo_ref.shape, F32)

    def emit(a_in, a_b, a_c):
        u = a_c * a_in
        y = a_b * _causal_conv3(u, halo_ref[...], cw_ref)
        halo_ref[...] = u[u.shape[0] - halo_ref.shape[0]:, :]
        o_ref[...] = y.astype(o_ref.dtype)
    _with_rounded_weights([(a_ref, win_ref, winbf_ref), (a_ref, wb_ref, wbbf_ref),
                           (a_ref, wc_ref, wcbf_ref)], emit)


def _conv_a(h, w, layer, conv_w, *, seq, d_conv, tm=1024, tn=256):
    t, k = h.shape
    ncb = d_conv // tn
    assert seq % tm == 0
    section = lambda s: pl.BlockSpec((None, k, tn), lambda j, i: (layer, 0, s * ncb + j))
    return pl.pallas_call(
        functools.partial(_conv_a_kernel, tiles_per_seq=seq // tm),
        grid=(ncb, t // tm),
        in_specs=[pl.BlockSpec((tm, k), lambda j, i: (i, 0)), section(0), section(1), section(2),
                  pl.BlockSpec((CONV_WIDTH, tn), lambda j, i: (0, j))],
        out_specs=pl.BlockSpec((tm, tn), lambda j, i: (i, j)),
        out_shape=jax.ShapeDtypeStruct((t, d_conv), BF16),
        scratch_shapes=[pltpu.VMEM((k, tn), BF16)] * 3 + [pltpu.VMEM((F32_SUBLANES, tn), F32)],
        compiler_params=_params(2),
        name="conv_a",
    )(h, w, w, w, conv_w)


def _attn_kernel(slopes_ref, lq1_ref, lk1_ref, lq2_ref, lk2_ref, g_ref, q_ref, k_ref, v_ref,
                 o_ref, vt_ref, bias_ref, u_ref, p_ref, alpha_ref, m_ref, acc_ref,
                 *, lam_init, blk, seq):
    h = pl.program_id(1)
    slope2 = slopes_ref[h] * LOG2E
    nq = seq // blk
    assert nq >= 2

    def xpose(c, carry):
        rows = pl.ds(pl.multiple_of(c * blk, blk), blk)
        vt_ref[0:HEAD_DIM_V, rows] = v_ref[rows, :].T
        return carry
    lax.fori_loop(0, nq, xpose, 0)
    pad_row = lax.broadcasted_iota(jnp.int32, (HALO_ROWS, seq), 0)
    vt_ref[HEAD_DIM_V:, :] = jnp.where(pad_row == 0, 1.0, 0.0).astype(vt_ref.dtype)
    krel = lax.broadcasted_iota(jnp.int32, (blk, blk), 0)
    qrel = lax.broadcasted_iota(jnp.int32, (blk, blk), 1)
    bias_ref[0] = slope2 * krel.astype(F32)
    visible = (krel // CHUNK) <= (qrel // CHUNK)
    bias_ref[1] = jnp.where(visible, slope2 * (qrel - jnp.abs(qrel - krel)).astype(F32), NEG_INF)

    def rows_of(j):
        if isinstance(j, int):
            return pl.ds(j * blk, blk)
        return pl.ds(pl.multiple_of(j * blk, blk), blk)

    def max_row_sq(ref):
        def body(c, best):
            x = ref[rows_of(c), :].astype(F32)
            sq = (x * x).reshape(blk // F32_SUBLANES, F32_SUBLANES, x.shape[1])
            return jnp.maximum(best, jnp.max(sq, axis=0))
        folded = lax.fori_loop(0, nq, body, jnp.zeros((F32_SUBLANES, 2 * HEAD_DIM_QK), F32))
        per_map = jnp.maximum(jnp.sum(folded[:, :HEAD_DIM_QK], axis=1, keepdims=True),
                              jnp.sum(folded[:, HEAD_DIM_QK:], axis=1, keepdims=True))
        return jnp.max(per_map, axis=0, keepdims=True)

    logit_bound = jnp.sqrt(max_row_sq(q_ref)) * jnp.sqrt(max_row_sq(k_ref))
    far_keys = (2.0 * logit_bound + UNDERFLOW_LOG2) / slope2
    reach = jnp.max(jnp.minimum(jnp.floor(far_keys * (1.0 / blk)), float(nq))).astype(jnp.int32) + 1
    reach = jnp.minimum(reach, nq)
    n_items = (reach + 1) * (reach + 2) // 2 + (nq - 1 - reach) * (reach + 1)
    n_items = jnp.where(reach >= nq - 1, nq * (nq + 1) // 2, n_items)

    lam = (jnp.exp(jnp.sum(lq1_ref[...] * lk1_ref[...], axis=-1, keepdims=True))
           - jnp.exp(jnp.sum(lq2_ref[...] * lk2_ref[...], axis=-1, keepdims=True))
           + lam_init)
    out_gain = g_ref[...] * (1.0 - lam_init)

    def first_block(qi):
        return jnp.maximum(qi - reach, 0)

    def next_item(item):
        qi, j = item
        last = j == qi
        return jnp.where(last, qi + 1, qi), jnp.where(last, first_block(qi + 1), j + 1)

    def scores(item, slot):
        qi, j = item
        k = k_ref[rows_of(j), :]
        q = q_ref[rows_of(qi), :]
        which = jnp.asarray(j == qi, jnp.int32)
        for c in range(2):
            kc = k[:, c * HEAD_DIM_QK:(c + 1) * HEAD_DIM_QK]
            qc = q[:, c * HEAD_DIM_QK:(c + 1) * HEAD_DIM_QK]
            u = lax.dot_general(kc, qc, (((1,), (1,)), ((), ())), preferred_element_type=F32)
            u_ref[slot, c] = u + bias_ref[which]

    def softmax(item, slot):
        qi, j = item
        off = slope2 * jnp.full((1, blk), (j - qi) * blk, jnp.int32).astype(F32)
        first = jnp.full((1, blk), j - first_block(qi), jnp.int32) == 0
        for c in range(2):
            m_old = jnp.where(first, NEG_INF, m_ref[c])
            m_new = jnp.maximum(m_old, jnp.max(u_ref[slot, c], axis=0, keepdims=True) + off)
            p_ref[slot, c] = jnp.exp2(u_ref[slot, c] - (m_new - off)).astype(p_ref.dtype)
            alpha_ref[slot, c] = jnp.exp2(m_old - m_new)
            m_ref[c] = m_new

    def values(item, slot):
        qi, j = item
        vt = vt_ref[:, rows_of(j)]
        for c in range(2):
            pv = jnp.dot(vt, p_ref[slot, c], preferred_element_type=F32)
            acc_ref[c] = alpha_ref[slot, c] * acc_ref[c] + pv

    def finish_query_block(qi):
        l0 = acc_ref[0, HEAD_DIM_V:HEAD_DIM_V + 1, :]
        l1 = acc_ref[1, HEAD_DIM_V:HEAD_DIM_V + 1, :]
        ot = (acc_ref[0, 0:HEAD_DIM_V, :] * (1.0 / l0)
              - acc_ref[1, 0:HEAD_DIM_V, :] * (lam / l1))
        ot = ot * lax.rsqrt(jnp.mean(ot * ot, axis=0, keepdims=True) + EPS)
        o_ref[rows_of(qi), :] = (ot.T * out_gain).astype(o_ref.dtype)

    def trip(items, slot):
        done, mid, ahead = items
        values(done, slot)
        softmax(mid, 1 - slot)
        scores(ahead, slot)

        @pl.when(done[1] == done[0])
        def _():
            finish_query_block(done[0])
        return mid, ahead, next_item(ahead)

    def trip_pair(i, items):
        return trip(trip(items, 0), 1)

    acc_ref[...] = jnp.zeros(acc_ref.shape, F32)
    m_ref[...] = jnp.full(m_ref.shape, NEG_INF, F32)
    zero = jnp.int32(0)
    item0, item1 = (zero, zero), (zero + 1, zero)
    scores(item0, 0)
    scores(item1, 1)
    softmax(item0, 0)
    n_trips = n_items - 2
    items = lax.fori_loop(0, n_trips // 2, trip_pair, (item0, item1, next_item(item1)))

    @pl.when(n_trips % 2 == 1)
    def _():
        trip(items, 0)

    last_slot = (n_items - 1) % 2
    values((nq - 1, nq - 2), 1 - last_slot)
    softmax((nq - 1, nq - 1), last_slot)
    values((nq - 1, nq - 1), last_slot)
    finish_query_block(nq - 1)


def _diff_attention(proj, lam_q1, lam_k1, lam_q2, lam_k2, subln_g, *, batch, seq, q_col, k_col,
                    v_col, lam_init, blk=512):
    t = proj.shape[0]
    hw = 2 * HEAD_DIM_QK
    assert hw == HEAD_DIM_V and seq % blk == 0 and blk % CHUNK == 0
    vt_rows = HEAD_DIM_V + HALO_ROWS
    slopes = 2.0 ** (-8.0 * jnp.arange(1, N_HEADS + 1, dtype=F32) / N_HEADS)
    vec = lambda n: pl.BlockSpec((1, n), lambda b, h, s: (0, 0))
    head_cols = lambda col: pl.BlockSpec((seq, hw), lambda b, h, s: (b, col // hw + h))
    grid_spec = pltpu.PrefetchScalarGridSpec(
        num_scalar_prefetch=1,
        grid=(batch, N_HEADS),
        in_specs=[vec(HEAD_DIM_QK), vec(HEAD_DIM_QK), vec(HEAD_DIM_QK), vec(HEAD_DIM_QK),
                  vec(HEAD_DIM_V), head_cols(q_col), head_cols(k_col), head_cols(v_col)],
        out_specs=pl.BlockSpec((seq, HEAD_DIM_V), lambda b, h, s: (b, h)),
        scratch_shapes=[pltpu.VMEM((vt_rows, seq), BF16),
                        pltpu.VMEM((2, blk, blk), F32),
                        pltpu.VMEM((2, 2, blk, blk), F32),
                        pltpu.VMEM((2, 2, blk, blk), BF16),
                        pltpu.VMEM((2, 2, 1, blk), F32),
                        pltpu.VMEM((2, 1, blk), F32),
                        pltpu.VMEM((2, vt_rows, blk), F32)],
    )
    return pl.pallas_call(
        functools.partial(_attn_kernel, lam_init=lam_init, blk=blk, seq=seq),
        grid_spec=grid_spec,
        out_shape=jax.ShapeDtypeStruct((t, N_HEADS * HEAD_DIM_V), BF16),
        compiler_params=_params(2),
        name="diff_attn",
    )(slopes, lam_q1.reshape(1, -1), lam_k1.reshape(1, -1), lam_q2.reshape(1, -1),
      lam_k2.reshape(1, -1), subln_g.reshape(1, -1), proj, proj, proj)


def _merge_kernel(ca_ref, at_ref, wa_ref, wb_ref, ga_ref, gb_ref, ba_ref, bb_ref, o_ref,
                  wabf_ref, wbbf_ref):
    def emit(y_a, y_b):
        g_a = jax.nn.sigmoid(ga_ref[...].astype(F32) + ba_ref[...])
        g_b = jax.nn.sigmoid(gb_ref[...].astype(F32) + bb_ref[...])
        o_ref[...] = (g_a * y_a + g_b * y_b).astype(o_ref.dtype)
    _with_rounded_weights([(ca_ref, wa_ref, wabf_ref), (at_ref, wb_ref, wbbf_ref)], emit)


def _merge(ca, attn, w_a, w_b, layer, proj, b_gate, *, gate_col, d_model, tm=1024, tn=512):
    t, ka = ca.shape
    kb = attn.shape[1]
    gcb = gate_col // tn
    ncb = d_model // tn
    bias = b_gate.reshape(1, -1)
    return pl.pallas_call(
        _merge_kernel,
        grid=(ncb, t // tm),
        in_specs=[pl.BlockSpec((tm, ka), lambda j, i: (i, 0)),
                  pl.BlockSpec((tm, kb), lambda j, i: (i, 0)),
                  pl.BlockSpec((None, ka, tn), lambda j, i: (layer, 0, j)),
                  pl.BlockSpec((None, kb, tn), lambda j, i: (layer, 0, j)),
                  pl.BlockSpec((tm, tn), lambda j, i: (i, gcb + j)),
                  pl.BlockSpec((tm, tn), lambda j, i: (i, gcb + ncb + j)),
                  pl.BlockSpec((1, tn), lambda j, i: (0, j)),
                  pl.BlockSpec((1, tn), lambda j, i: (0, ncb + j))],
        out_specs=pl.BlockSpec((tm, tn), lambda j, i: (i, j)),
        out_shape=jax.ShapeDtypeStruct((t, d_model), BF16),
        scratch_shapes=[pltpu.VMEM((ka, tn), BF16), pltpu.VMEM((kb, tn), BF16)],
        compiler_params=_params(2),
        name="merge",
    )(ca, attn, w_a, w_b, proj, proj, bias, bias)


def _ffn_in_kernel(a_ref, wg_ref, wu_ref, cw_ref, o_ref, wgbf_ref, wubf_ref, halo_ref,
                   *, tiles_per_seq):
    @pl.when(pl.program_id(1) % tiles_per_seq == 0)
    def _sequence_start():
        halo_ref[...] = jnp.zeros(halo_ref.shape, F32)

    def emit(gate, up):
        z = _causal_conv3(gate, halo_ref[...], cw_ref)
        halo_ref[...] = gate[gate.shape[0] - halo_ref.shape[0]:, :]
        c0 = math.sqrt(2.0 / math.pi)
        act = 0.5 * z * (1.0 + jnp.tanh(c0 * (z + 0.044715 * (z * z * z))))
        o_ref[...] = (act * up).astype(o_ref.dtype)
    _with_rounded_weights([(a_ref, wg_ref, wgbf_ref), (a_ref, wu_ref, wubf_ref)], emit)


def _ffn_in(h, w, layer, conv_w, *, seq, d_ff, tm=1024, tn=256):
    t, k = h.shape
    ncb = d_ff // tn
    assert seq % tm == 0
    return pl.pallas_call(
        functools.partial(_ffn_in_kernel, tiles_per_seq=seq // tm),
        grid=(ncb, t // tm),
        in_specs=[pl.BlockSpec((tm, k), lambda j, i: (i, 0)),
                  pl.BlockSpec((None, k, tn), lambda j, i: (layer, 0, j)),
                  pl.BlockSpec((None, k, tn), lambda j, i: (layer, 0, ncb + j)),
                  pl.BlockSpec((CONV_WIDTH, tn), lambda j, i: (0, j))],
        out_specs=pl.BlockSpec((tm, tn), lambda j, i: (i, j)),
        out_shape=jax.ShapeDtypeStruct((t, d_ff), BF16),
        scratch_shapes=[pltpu.VMEM((k, tn), BF16), pltpu.VMEM((k, tn), BF16),
                        pltpu.VMEM((F32_SUBLANES, tn), F32)],
        compiler_params=_params(2),
        name="ffn_in",
    )(h, w, w, conv_w)


def _lambda_init(layer_idx):
    return 0.8 - 0.6 * math.exp(-0.3 * layer_idx)


def kernel(x, w_in, b_gate, conv_a, w_a_out, lam_q1, lam_k1, lam_q2, lam_k2, subln_g, w_b_out, w_o,
           norm_mix_pre, norm_mix_post, w_ffn_in, conv_ffn, w_ffn_out, norm_ffn_pre, norm_ffn_post):
    batch, seq, d_model = x.shape
    depth = w_in.shape[0]
    d_conv = conv_a.shape[-1]
    d_ff = conv_ffn.shape[-1]
    d_qk = N_HEADS * 2 * HEAD_DIM_QK
    proj_start = 3 * d_conv
    q_col = 0
    k_col = q_col + d_qk
    v_col = k_col + d_qk
    gate_col = v_col + N_HEADS * HEAD_DIM_V

    d_in = w_in.shape[-1]
    q_scale = jnp.ones((d_in,), F32).at[proj_start + q_col:proj_start + k_col].set(QK_SCALE_LOG2)

    w_ffn_out_bf = w_ffn_out.astype(BF16)

    xt = x.reshape(batch * seq, d_model)
    h = _norm(xt, norm_mix_pre[0])
    for l in range(depth):
        ca = _conv_a(h, w_in, l, conv_a[l], seq=seq, d_conv=d_conv)
        proj = _matmul(h, w_in, l, BF16, tm=1024, tn=1024, name="in_proj", col_scale=q_scale,
                       w_buffers=1, col_start=proj_start)
        attn = _diff_attention(proj, lam_q1[l], lam_k1[l], lam_q2[l], lam_k2[l], subln_g[l],
                               batch=batch, seq=seq, q_col=q_col, k_col=k_col, v_col=v_col,
                               lam_init=_lambda_init(l))
        merged = _merge(ca, attn, w_a_out, w_b_out, l, proj, b_gate[l],
                        gate_col=gate_col, d_model=d_model)
        mix = _matmul(merged, w_o, l, BF16, tm=1024, tn=1024, name="w_o", w_buffers=1)
        xt, h2 = _resid_norm(xt, mix, norm_mix_post[l], norm_ffn_pre[l])

        f = _ffn_in(h2, w_ffn_in, l, conv_ffn[l], seq=seq, d_ff=d_ff)
        ffn = _matmul(f, w_ffn_out_bf, l, BF16, tm=512, tn=512, name="ffn_out")
        g_next = norm_mix_pre[l + 1] if l + 1 < depth else None
        xt, h = _resid_norm(xt, ffn, norm_ffn_post[l], g_next)
    return xt.reshape(batch, seq, d_model)
```

```python
import functools
import math

import jax
import jax.numpy as jnp
from jax import lax
from jax.experimental import pallas as pl
from jax.experimental.pallas import tpu as pltpu

CHUNK = 64
CONV_WIDTH = 3
N_HEADS = 8
HEAD_DIM_QK = 128
HEAD_DIM_V = 256
EPS = 1e-6
NEG_INF = -1e30
UNDERFLOW_LOG2 = 150.0
LOG2E = math.log2(math.e)
QK_SCALE_LOG2 = HEAD_DIM_QK ** -0.5 * LOG2E

VMEM_LIMIT_BYTES = 56 * 1024 * 1024
HALO_ROWS = 16
F32_SUBLANES = 8
W_CAST_CHUNKS = 4

NORM_ROWS = 256
CONV_A_TILE = (1024, 256)
IN_PROJ_TILE = (512, 1024)
MERGE_TILE = (1024, 512)
W_O_TILE = (512, 1024)
FFN_IN_TILE = (1024, 256)
FFN_OUT_TILE = (512, 512)
ATTN_BLOCK = 512

F32 = jnp.float32
BF16 = jnp.bfloat16


def _params(n_grid_dims):
    return pltpu.CompilerParams(
        dimension_semantics=("arbitrary",) * n_grid_dims,
        vmem_limit_bytes=VMEM_LIMIT_BYTES)


def _rms(x, g):
    return x * lax.rsqrt(jnp.mean(x * x, axis=-1, keepdims=True) + EPS) * g


def _norm_kernel(x_ref, g_ref, h_ref):
    h_ref[...] = _rms(x_ref[...], g_ref[...]).astype(h_ref.dtype)


def _norm(x, g, *, rows=NORM_ROWS):
    t, d = x.shape
    return pl.pallas_call(
        _norm_kernel,
        grid=(t // rows,),
        in_specs=[pl.BlockSpec((rows, d), lambda i: (i, 0)),
                  pl.BlockSpec((1, d), lambda i: (0, 0))],
        out_specs=pl.BlockSpec((rows, d), lambda i: (i, 0)),
        out_shape=jax.ShapeDtypeStruct((t, d), BF16),
        compiler_params=_params(1),
        name="norm",
    )(x, g.reshape(1, d))


def _resid_norm_kernel(x_ref, y_ref, gpost_ref, gpre_ref, xo_ref, h_ref):
    x_new = x_ref[...] + _rms(y_ref[...].astype(F32), gpost_ref[...])
    xo_ref[...] = x_new
    h_ref[...] = _rms(x_new, gpre_ref[...]).astype(h_ref.dtype)


def _resid_kernel(x_ref, y_ref, gpost_ref, xo_ref):
    xo_ref[...] = x_ref[...] + _rms(y_ref[...].astype(F32), gpost_ref[...])


def _resid_norm(x, y, g_post, g_pre_next, *, rows=NORM_ROWS):
    t, d = x.shape
    row_spec = pl.BlockSpec((rows, d), lambda i: (i, 0))
    vec_spec = pl.BlockSpec((1, d), lambda i: (0, 0))
    if g_pre_next is None:
        return pl.pallas_call(
            _resid_kernel,
            grid=(t // rows,),
            in_specs=[row_spec, row_spec, vec_spec],
            out_specs=row_spec,
            out_shape=jax.ShapeDtypeStruct((t, d), F32),
            compiler_params=_params(1),
            name="resid",
        )(x, y, g_post.reshape(1, d)), None
    return pl.pallas_call(
        _resid_norm_kernel,
        grid=(t // rows,),
        in_specs=[row_spec, row_spec, vec_spec, vec_spec],
        out_specs=[row_spec, row_spec],
        out_shape=[jax.ShapeDtypeStruct((t, d), F32), jax.ShapeDtypeStruct((t, d), BF16)],
        compiler_params=_params(1),
        name="resid_norm",
    )(x, y, g_post.reshape(1, d), g_pre_next.reshape(1, d))


def _dot_rounding_in_chunks(a_ref, w_ref, wbf_ref):
    kc = w_ref.shape[0] // W_CAST_CHUNKS
    acc = None
    for c in range(W_CAST_CHUNKS):
        ks = pl.ds(c * kc, kc)
        wbf_ref[ks, :] = w_ref[ks, :].astype(wbf_ref.dtype)
        part = jnp.dot(a_ref[:, ks], wbf_ref[ks, :], preferred_element_type=F32)
        acc = part if acc is None else acc + part
    return acc


def _with_rounded_weights(operands, emit):
    @pl.when(pl.program_id(1) == 0)
    def _first_token_tile():
        emit(*[_dot_rounding_in_chunks(*op) for op in operands])

    @pl.when(pl.program_id(1) != 0)
    def _later_token_tiles():
        emit(*[jnp.dot(a_ref[...], wbf_ref[...], preferred_element_type=F32)
               for a_ref, _, wbf_ref in operands])


def _mm_direct_kernel(a_ref, w_ref, o_ref):
    o_ref[...] = jnp.dot(a_ref[...], w_ref[...], preferred_element_type=F32).astype(o_ref.dtype)


def _mm_kernel(a_ref, w_ref, o_ref, wbf_ref):
    def emit(acc):
        o_ref[...] = acc.astype(o_ref.dtype)
    _with_rounded_weights([(a_ref, w_ref, wbf_ref)], emit)


def _mm_colscale_kernel(a_ref, w_ref, s_ref, o_ref, wbf_ref):
    def emit(acc):
        o_ref[...] = (acc * s_ref[...]).astype(o_ref.dtype)
    _with_rounded_weights([(a_ref, w_ref, wbf_ref)], emit)


def _matmul(a, w, layer, out_dtype, *, tile, name, col_scale=None, col_start=0):
    tm, tn = tile
    t, k = a.shape
    n = w.shape[2] - col_start
    c0 = col_start // tn
    assert col_start % tn == 0
    in_specs = [pl.BlockSpec((tm, k), lambda j, i: (i, 0)),
                pl.BlockSpec((None, k, tn), lambda j, i: (layer, 0, c0 + j))]
    operands = [a, w]
    if col_scale is not None:
        in_specs.append(pl.BlockSpec((1, tn), lambda j, i: (0, c0 + j)))
        operands.append(col_scale.reshape(1, -1))
    if w.dtype == BF16:
        assert col_scale is None
        body, scratch = _mm_direct_kernel, []
    else:
        body = _mm_kernel if col_scale is None else _mm_colscale_kernel
        scratch = [pltpu.VMEM((k, tn), BF16)]
    return pl.pallas_call(
        body,
        grid=(n // tn, t // tm),
        in_specs=in_specs,
        out_specs=pl.BlockSpec((tm, tn), lambda j, i: (i, j)),
        out_shape=jax.ShapeDtypeStruct((t, n), out_dtype),
        scratch_shapes=scratch,
        compiler_params=_params(2),
        name=name,
    )(*operands)


def _causal_conv3(u, halo, w_ref):
    rows, pad = u.shape[0], halo.shape[0]
    ext = jnp.concatenate([halo, u], axis=0)
    u1 = pltpu.roll(ext, 1, 0)[pad:pad + rows]
    u2 = pltpu.roll(ext, 2, 0)[pad:pad + rows]
    return w_ref[0:1, :] * u2 + w_ref[1:2, :] * u1 + w_ref[2:3, :] * u


def _conv_a_kernel(a_ref, win_ref, wb_ref, wc_ref, cw_ref, o_ref, winbf_ref, wbbf_ref, wcbf_ref,
                   halo_ref, *, tiles_per_seq):
    @pl.when(pl.program_id(1) % tiles_per_seq == 0)
    def _sequence_start():
        halo_ref[...] = jnp.zeros(halo_ref.shape, F32)

    def emit(a_in, a_b, a_c):
        u = a_c * a_in
        y = a_b * _causal_conv3(u, halo_ref[...], cw_ref)
        halo_ref[...] = u[u.shape[0] - halo_ref.shape[0]:, :]
        o_ref[...] = y.astype(o_ref.dtype)
    _with_rounded_weights([(a_ref, win_ref, winbf_ref), (a_ref, wb_ref, wbbf_ref),
                           (a_ref, wc_ref, wcbf_ref)], emit)


def _conv_a(h, w, layer, conv_w, *, seq, d_conv, tile=CONV_A_TILE):
    tm, tn = tile
    t, k = h.shape
    ncb = d_conv // tn
    assert seq % tm == 0
    section = lambda s: pl.BlockSpec((None, k, tn), lambda j, i: (layer, 0, s * ncb + j))
    return pl.pallas_call(
        functools.partial(_conv_a_kernel, tiles_per_seq=seq // tm),
        grid=(ncb, t // tm),
        in_specs=[pl.BlockSpec((tm, k), lambda j, i: (i, 0)), section(0), section(1), section(2),
                  pl.BlockSpec((CONV_WIDTH, tn), lambda j, i: (0, j))],
        out_specs=pl.BlockSpec((tm, tn), lambda j, i: (i, j)),
        out_shape=jax.ShapeDtypeStruct((t, d_conv), BF16),
        scratch_shapes=[pltpu.VMEM((k, tn), BF16)] * 3 + [pltpu.VMEM((F32_SUBLANES, tn), F32)],
        compiler_params=_params(2),
        name="conv_a",
    )(h, w, w, w, conv_w)


def _attn_kernel(slopes_ref, lq1_ref, lk1_ref, lq2_ref, lk2_ref, g_ref, q_ref, k_ref, v_ref,
                 o_ref, vt_ref, bias_ref, u_ref, p_ref, alpha_ref, m_ref, acc_ref,
                 *, lam_init, blk, seq):
    h = pl.program_id(1)
    slope2 = slopes_ref[h] * LOG2E
    nq = seq // blk
    assert nq >= 2

    def xpose(c, carry):
        rows = pl.ds(pl.multiple_of(c * blk, blk), blk)
        vt_ref[0:HEAD_DIM_V, rows] = v_ref[rows, :].T
        return carry
    lax.fori_loop(0, nq, xpose, 0)
    pad_row = lax.broadcasted_iota(jnp.int32, (HALO_ROWS, seq), 0)
    vt_ref[HEAD_DIM_V:, :] = jnp.where(pad_row == 0, 1.0, 0.0).astype(vt_ref.dtype)
    krel = lax.broadcasted_iota(jnp.int32, (blk, blk), 0)
    qrel = lax.broadcasted_iota(jnp.int32, (blk, blk), 1)
    bias_ref[0] = slope2 * krel.astype(F32)
    visible = (krel // CHUNK) <= (qrel // CHUNK)
    bias_ref[1] = jnp.where(visible, slope2 * (qrel - jnp.abs(qrel - krel)).astype(F32), NEG_INF)

    def rows_of(j):
        if isinstance(j, int):
            return pl.ds(j * blk, blk)
        return pl.ds(pl.multiple_of(j * blk, blk), blk)

    def max_row_sq(ref):
        def body(c, best):
            x = ref[rows_of(c), :].astype(F32)
            sq = (x * x).reshape(blk // F32_SUBLANES, F32_SUBLANES, x.shape[1])
            return jnp.maximum(best, jnp.max(sq, axis=0))
        folded = lax.fori_loop(0, nq, body, jnp.zeros((F32_SUBLANES, 2 * HEAD_DIM_QK), F32))
        per_map = jnp.maximum(jnp.sum(folded[:, :HEAD_DIM_QK], axis=1, keepdims=True),
                              jnp.sum(folded[:, HEAD_DIM_QK:], axis=1, keepdims=True))
        return jnp.max(per_map, axis=0, keepdims=True)

    logit_bound = jnp.sqrt(max_row_sq(q_ref)) * jnp.sqrt(max_row_sq(k_ref))
    far_keys = (2.0 * logit_bound + UNDERFLOW_LOG2) / slope2
    reach = jnp.max(jnp.minimum(jnp.floor(far_keys * (1.0 / blk)), float(nq))).astype(jnp.int32) + 1
    reach = jnp.minimum(reach, nq)
    n_items = (reach + 1) * (reach + 2) // 2 + (nq - 1 - reach) * (reach + 1)
    n_items = jnp.where(reach >= nq - 1, nq * (nq + 1) // 2, n_items)

    lam = (jnp.exp(jnp.sum(lq1_ref[...] * lk1_ref[...], axis=-1, keepdims=True))
           - jnp.exp(jnp.sum(lq2_ref[...] * lk2_ref[...], axis=-1, keepdims=True))
           + lam_init)
    out_gain = g_ref[...] * (1.0 - lam_init)

    def first_block(qi):
        return jnp.maximum(qi - reach, 0)

    def next_item(item):
        qi, j = item
        last = j == qi
        return jnp.where(last, qi + 1, qi), jnp.where(last, first_block(qi + 1), j + 1)

    def scores(item, slot):
        qi, j = item
        k = k_ref[rows_of(j), :]
        q = q_ref[rows_of(qi), :]
        which = jnp.asarray(j == qi, jnp.int32)
        for c in range(2):
            kc = k[:, c * HEAD_DIM_QK:(c + 1) * HEAD_DIM_QK]
            qc = q[:, c * HEAD_DIM_QK:(c + 1) * HEAD_DIM_QK]
            u = lax.dot_general(kc, qc, (((1,), (1,)), ((), ())), preferred_element_type=F32)
            u_ref[slot, c] = u + bias_ref[which]

    def softmax(item, slot):
        qi, j = item
        off = slope2 * jnp.full((1, blk), (j - qi) * blk, jnp.int32).astype(F32)
        first = jnp.full((1, blk), j - first_block(qi), jnp.int32) == 0
        for c in range(2):
            m_old = jnp.where(first, NEG_INF, m_ref[c])
            m_new = jnp.maximum(m_old, jnp.max(u_ref[slot, c], axis=0, keepdims=True) + off)
            p_ref[slot, c] = jnp.exp2(u_ref[slot, c] - (m_new - off)).astype(p_ref.dtype)
            alpha_ref[slot, c] = jnp.exp2(m_old - m_new)
            m_ref[c] = m_new

    def values(item, slot):
        qi, j = item
        vt = vt_ref[:, rows_of(j)]
        for c in range(2):
            pv = jnp.dot(vt, p_ref[slot, c], preferred_element_type=F32)
            acc_ref[c] = alpha_ref[slot, c] * acc_ref[c] + pv

    def finish_query_block(qi):
        l0 = acc_ref[0, HEAD_DIM_V:HEAD_DIM_V + 1, :]
        l1 = acc_ref[1, HEAD_DIM_V:HEAD_DIM_V + 1, :]
        ot = (acc_ref[0, 0:HEAD_DIM_V, :] * (1.0 / l0)
              - acc_ref[1, 0:HEAD_DIM_V, :] * (lam / l1))
        ot = ot * lax.rsqrt(jnp.mean(ot * ot, axis=0, keepdims=True) + EPS)
        o_ref[rows_of(qi), :] = (ot.T * out_gain).astype(o_ref.dtype)

    def trip(items, slot):
        done, mid, ahead = items
        values(done, slot)
        softmax(mid, 1 - slot)
        scores(ahead, slot)

        @pl.when(done[1] == done[0])
        def _():
            finish_query_block(done[0])
        return mid, ahead, next_item(ahead)

    def trip_pair(i, items):
        return trip(trip(items, 0), 1)

    acc_ref[...] = jnp.zeros(acc_ref.shape, F32)
    m_ref[...] = jnp.full(m_ref.shape, NEG_INF, F32)
    zero = jnp.int32(0)
    item0, item1 = (zero, zero), (zero + 1, zero)
    scores(item0, 0)
    scores(item1, 1)
    softmax(item0, 0)
    n_trips = n_items - 2
    items = lax.fori_loop(0, n_trips // 2, trip_pair, (item0, item1, next_item(item1)))

    @pl.when(n_trips % 2 == 1)
    def _():
        trip(items, 0)

    last_slot = (n_items - 1) % 2
    values((nq - 1, nq - 2), 1 - last_slot)
    softmax((nq - 1, nq - 1), last_slot)
    values((nq - 1, nq - 1), last_slot)
    finish_query_block(nq - 1)


def _diff_attention(proj, lam_q1, lam_k1, lam_q2, lam_k2, subln_g, *, batch, seq, q_col, k_col,
                    v_col, lam_init, blk=ATTN_BLOCK):
    t = proj.shape[0]
    hw = 2 * HEAD_DIM_QK
    assert hw == HEAD_DIM_V and seq % blk == 0 and blk % CHUNK == 0
    vt_rows = HEAD_DIM_V + HALO_ROWS
    slopes = 2.0 ** (-8.0 * jnp.arange(1, N_HEADS + 1, dtype=F32) / N_HEADS)
    vec = lambda n: pl.BlockSpec((1, n), lambda b, h, s: (0, 0))
    head_cols = lambda col: pl.BlockSpec((seq, hw), lambda b, h, s: (b, col // hw + h))
    grid_spec = pltpu.PrefetchScalarGridSpec(
        num_scalar_prefetch=1,
        grid=(batch, N_HEADS),
        in_specs=[vec(HEAD_DIM_QK), vec(HEAD_DIM_QK), vec(HEAD_DIM_QK), vec(HEAD_DIM_QK),
                  vec(HEAD_DIM_V), head_cols(q_col), head_cols(k_col), head_cols(v_col)],
        out_specs=pl.BlockSpec((seq, HEAD_DIM_V), lambda b, h, s: (b, h)),
        scratch_shapes=[pltpu.VMEM((vt_rows, seq), BF16),
                        pltpu.VMEM((2, blk, blk), F32),
                        pltpu.VMEM((2, 2, blk, blk), F32),
                        pltpu.VMEM((2, 2, blk, blk), BF16),
                        pltpu.VMEM((2, 2, 1, blk), F32),
                        pltpu.VMEM((2, 1, blk), F32),
                        pltpu.VMEM((2, vt_rows, blk), F32)],
    )
    return pl.pallas_call(
        functools.partial(_attn_kernel, lam_init=lam_init, blk=blk, seq=seq),
        grid_spec=grid_spec,
        out_shape=jax.ShapeDtypeStruct((t, N_HEADS * HEAD_DIM_V), BF16),
        compiler_params=_params(2),
        name="diff_attn",
    )(slopes, lam_q1.reshape(1, -1), lam_k1.reshape(1, -1), lam_q2.reshape(1, -1),
      lam_k2.reshape(1, -1), subln_g.reshape(1, -1), proj, proj, proj)


def _merge_kernel(ca_ref, at_ref, wa_ref, wb_ref, ga_ref, gb_ref, ba_ref, bb_ref, o_ref,
                  wabf_ref, wbbf_ref):
    def emit(y_a, y_b):
        g_a = jax.nn.sigmoid(ga_ref[...].astype(F32) + ba_ref[...])
        g_b = jax.nn.sigmoid(gb_ref[...].astype(F32) + bb_ref[...])
        o_ref[...] = (g_a * y_a + g_b * y_b).astype(o_ref.dtype)
    _with_rounded_weights([(ca_ref, wa_ref, wabf_ref), (at_ref, wb_ref, wbbf_ref)], emit)


def _merge(ca, attn, w_a, w_b, layer, proj, b_gate, *, gate_col, d_model, tile=MERGE_TILE):
    tm, tn = tile
    t, ka = ca.shape
    kb = attn.shape[1]
    gcb = gate_col // tn
    ncb = d_model // tn
    bias = b_gate.reshape(1, -1)
    return pl.pallas_call(
        _merge_kernel,
        grid=(ncb, t // tm),
        in_specs=[pl.BlockSpec((tm, ka), lambda j, i: (i, 0)),
                  pl.BlockSpec((tm, kb), lambda j, i: (i, 0)),
                  pl.BlockSpec((None, ka, tn), lambda j, i: (layer, 0, j)),
                  pl.BlockSpec((None, kb, tn), lambda j, i: (layer, 0, j)),
                  pl.BlockSpec((tm, tn), lambda j, i: (i, gcb + j)),
                  pl.BlockSpec((tm, tn), lambda j, i: (i, gcb + ncb + j)),
                  pl.BlockSpec((1, tn), lambda j, i: (0, j)),
                  pl.BlockSpec((1, tn), lambda j, i: (0, ncb + j))],
        out_specs=pl.BlockSpec((tm, tn), lambda j, i: (i, j)),
        out_shape=jax.ShapeDtypeStruct((t, d_model), BF16),
        scratch_shapes=[pltpu.VMEM((ka, tn), BF16), pltpu.VMEM((kb, tn), BF16)],
        compiler_params=_params(2),
        name="merge",
    )(ca, attn, w_a, w_b, proj, proj, bias, bias)


def _ffn_in_kernel(a_ref, wg_ref, wu_ref, cw_ref, o_ref, wgbf_ref, wubf_ref, halo_ref,
                   *, tiles_per_seq):
    @pl.when(pl.program_id(1) % tiles_per_seq == 0)
    def _sequence_start():
        halo_ref[...] = jnp.zeros(halo_ref.shape, F32)

    def emit(gate, up):
        z = _causal_conv3(gate, halo_ref[...], cw_ref)
        halo_ref[...] = gate[gate.shape[0] - halo_ref.shape[0]:, :]
        c0 = math.sqrt(2.0 / math.pi)
        act = 0.5 * z * (1.0 + jnp.tanh(c0 * (z + 0.044715 * (z * z * z))))
        o_ref[...] = (act * up).astype(o_ref.dtype)
    _with_rounded_weights([(a_ref, wg_ref, wgbf_ref), (a_ref, wu_ref, wubf_ref)], emit)


def _ffn_in(h, w, layer, conv_w, *, seq, d_ff, tile=FFN_IN_TILE):
    tm, tn = tile
    t, k = h.shape
    ncb = d_ff // tn
    assert seq % tm == 0
    return pl.pallas_call(
        functools.partial(_ffn_in_kernel, tiles_per_seq=seq // tm),
        grid=(ncb, t // tm),
        in_specs=[pl.BlockSpec((tm, k), lambda j, i: (i, 0)),
                  pl.BlockSpec((None, k, tn), lambda j, i: (layer, 0, j)),
                  pl.BlockSpec((None, k, tn), lambda j, i: (layer, 0, ncb + j)),
                  pl.BlockSpec((CONV_WIDTH, tn), lambda j, i: (0, j))],
        out_specs=pl.BlockSpec((tm, tn), lambda j, i: (i, j)),
        out_shape=jax.ShapeDtypeStruct((t, d_ff), BF16),
        scratch_shapes=[pltpu.VMEM((k, tn), BF16), pltpu.VMEM((k, tn), BF16),
                        pltpu.VMEM((F32_SUBLANES, tn), F32)],
        compiler_params=_params(2),
        name="ffn_in",
    )(h, w, w, conv_w)


def _lambda_init(layer_idx):
    return 0.8 - 0.6 * math.exp(-0.3 * layer_idx)


def kernel(x, w_in, b_gate, conv_a, w_a_out, lam_q1, lam_k1, lam_q2, lam_k2, subln_g, w_b_out, w_o,
           norm_mix_pre, norm_mix_post, w_ffn_in, conv_ffn, w_ffn_out, norm_ffn_pre, norm_ffn_post):
    batch, seq, d_model = x.shape
    depth = w_in.shape[0]
    d_conv = conv_a.shape[-1]
    d_ff = conv_ffn.shape[-1]
    d_qk = N_HEADS * 2 * HEAD_DIM_QK
    proj_start = 3 * d_conv
    q_col = 0
    k_col = q_col + d_qk
    v_col = k_col + d_qk
    gate_col = v_col + N_HEADS * HEAD_DIM_V

    d_in = w_in.shape[-1]
    q_scale = jnp.ones((d_in,), F32).at[proj_start + q_col:proj_start + k_col].set(QK_SCALE_LOG2)

    w_ffn_out_bf = w_ffn_out.astype(BF16)

    xt = x.reshape(batch * seq, d_model)
    h = _norm(xt, norm_mix_pre[0])
    for l in range(depth):
        ca = _conv_a(h, w_in, l, conv_a[l], seq=seq, d_conv=d_conv)
        proj = _matmul(h, w_in, l, BF16, tile=IN_PROJ_TILE, name="in_proj", col_scale=q_scale,
                       col_start=proj_start)
        attn = _diff_attention(proj, lam_q1[l], lam_k1[l], lam_q2[l], lam_k2[l], subln_g[l],
                               batch=batch, seq=seq, q_col=q_col, k_col=k_col, v_col=v_col,
                               lam_init=_lambda_init(l))
        merged = _merge(ca, attn, w_a_out, w_b_out, l, proj, b_gate[l],
                        gate_col=gate_col, d_model=d_model)
        mix = _matmul(merged, w_o, l, BF16, tile=W_O_TILE, name="w_o")
        xt, h2 = _resid_norm(xt, mix, norm_mix_post[l], norm_ffn_pre[l])

        f = _ffn_in(h2, w_ffn_in, l, conv_ffn[l], seq=seq, d_ff=d_ff)
        ffn = _matmul(f, w_ffn_out_bf, l, BF16, tile=FFN_OUT_TILE, name="ffn_out")
        g_next = norm_mix_pre[l + 1] if l + 1 < depth else None
        xt, h = _resid_norm(xt, ffn, norm_ffn_post[l], g_next)
    return xt.reshape(batch, seq, d_model)
```

```python
import functools
import math

import jax
import jax.numpy as jnp
from jax import lax
from jax.experimental import pallas as pl
from jax.experimental.pallas import tpu as pltpu

CHUNK = 64
CONV_WIDTH = 3
N_HEADS = 8
HEAD_DIM_QK = 128
HEAD_DIM_V = 256
EPS = 1e-6
NEG_INF = -1e30
UNDERFLOW_LOG2 = 150.0
LOG2E = math.log2(math.e)
QK_SCALE_LOG2 = HEAD_DIM_QK ** -0.5 * LOG2E

VMEM_LIMIT_BYTES = 56 * 1024 * 1024
BF16_SUBLANES = 16
F32_SUBLANES = 8
W_CAST_CHUNKS = 4

NORM_ROWS = 256
CONV_A_TILE = (1024, 256)
IN_PROJ_TILE = (512, 1024)
MERGE_TILE = (1024, 512)
W_O_TILE = (512, 1024)
FFN_IN_TILE = (1024, 256)
FFN_OUT_TILE = (512, 512)
ATTN_BLOCK = 512

F32 = jnp.float32
BF16 = jnp.bfloat16


def _params(n_grid_dims):
    return pltpu.CompilerParams(
        dimension_semantics=("arbitrary",) * n_grid_dims,
        vmem_limit_bytes=VMEM_LIMIT_BYTES)


def _rms(x, g):
    return x * lax.rsqrt(jnp.mean(x * x, axis=-1, keepdims=True) + EPS) * g


def _norm_kernel(x_ref, g_ref, h_ref):
    h_ref[...] = _rms(x_ref[...], g_ref[...]).astype(h_ref.dtype)


def _norm(x, g, *, rows=NORM_ROWS):
    t, d = x.shape
    return pl.pallas_call(
        _norm_kernel,
        grid=(t // rows,),
        in_specs=[pl.BlockSpec((rows, d), lambda i: (i, 0)),
                  pl.BlockSpec((1, d), lambda i: (0, 0))],
        out_specs=pl.BlockSpec((rows, d), lambda i: (i, 0)),
        out_shape=jax.ShapeDtypeStruct((t, d), BF16),
        compiler_params=_params(1),
        name="norm",
    )(x, g.reshape(1, d))


def _resid_norm_kernel(x_ref, y_ref, gpost_ref, gpre_ref, xo_ref, h_ref):
    x_new = x_ref[...] + _rms(y_ref[...].astype(F32), gpost_ref[...])
    xo_ref[...] = x_new
    h_ref[...] = _rms(x_new, gpre_ref[...]).astype(h_ref.dtype)


def _resid_kernel(x_ref, y_ref, gpost_ref, xo_ref):
    xo_ref[...] = x_ref[...] + _rms(y_ref[...].astype(F32), gpost_ref[...])


def _resid_norm(x, y, g_post, g_pre_next, *, rows=NORM_ROWS):
    t, d = x.shape
    row_spec = pl.BlockSpec((rows, d), lambda i: (i, 0))
    vec_spec = pl.BlockSpec((1, d), lambda i: (0, 0))
    if g_pre_next is None:
        return pl.pallas_call(
            _resid_kernel,
            grid=(t // rows,),
            in_specs=[row_spec, row_spec, vec_spec],
            out_specs=row_spec,
            out_shape=jax.ShapeDtypeStruct((t, d), F32),
            compiler_params=_params(1),
            name="resid",
        )(x, y, g_post.reshape(1, d)), None
    return pl.pallas_call(
        _resid_norm_kernel,
        grid=(t // rows,),
        in_specs=[row_spec, row_spec, vec_spec, vec_spec],
        out_specs=[row_spec, row_spec],
        out_shape=[jax.ShapeDtypeStruct((t, d), F32), jax.ShapeDtypeStruct((t, d), BF16)],
        compiler_params=_params(1),
        name="resid_norm",
    )(x, y, g_post.reshape(1, d), g_pre_next.reshape(1, d))


def _dot_rounding_in_chunks(a_ref, w_ref, wbf_ref):
    kc = w_ref.shape[0] // W_CAST_CHUNKS
    acc = None
    for c in range(W_CAST_CHUNKS):
        ks = pl.ds(c * kc, kc)
        wbf_ref[ks, :] = w_ref[ks, :].astype(wbf_ref.dtype)
        part = jnp.dot(a_ref[:, ks], wbf_ref[ks, :], preferred_element_type=F32)
        acc = part if acc is None else acc + part
    return acc


def _with_rounded_weights(operands, emit):
    @pl.when(pl.program_id(1) == 0)
    def _first_token_tile():
        emit(*[_dot_rounding_in_chunks(*op) for op in operands])

    @pl.when(pl.program_id(1) != 0)
    def _later_token_tiles():
        emit(*[jnp.dot(a_ref[...], wbf_ref[...], preferred_element_type=F32)
               for a_ref, _, wbf_ref in operands])


def _mm_direct_kernel(a_ref, w_ref, o_ref):
    o_ref[...] = jnp.dot(a_ref[...], w_ref[...], preferred_element_type=F32).astype(o_ref.dtype)


def _mm_kernel(a_ref, w_ref, o_ref, wbf_ref):
    def emit(acc):
        o_ref[...] = acc.astype(o_ref.dtype)
    _with_rounded_weights([(a_ref, w_ref, wbf_ref)], emit)


def _mm_colscale_kernel(a_ref, w_ref, s_ref, o_ref, wbf_ref):
    def emit(acc):
        o_ref[...] = (acc * s_ref[...]).astype(o_ref.dtype)
    _with_rounded_weights([(a_ref, w_ref, wbf_ref)], emit)


def _matmul(a, w, layer, out_dtype, *, tile, name, col_scale=None, col_start=0):
    tm, tn = tile
    t, k = a.shape
    n = w.shape[2] - col_start
    c0 = col_start // tn
    assert col_start % tn == 0
    in_specs = [pl.BlockSpec((tm, k), lambda j, i: (i, 0)),
                pl.BlockSpec((None, k, tn), lambda j, i: (layer, 0, c0 + j))]
    operands = [a, w]
    if col_scale is not None:
        in_specs.append(pl.BlockSpec((1, tn), lambda j, i: (0, c0 + j)))
        operands.append(col_scale.reshape(1, -1))
    if w.dtype == BF16:
        assert col_scale is None
        body, scratch = _mm_direct_kernel, []
    else:
        body = _mm_kernel if col_scale is None else _mm_colscale_kernel
        scratch = [pltpu.VMEM((k, tn), BF16)]
    return pl.pallas_call(
        body,
        grid=(n // tn, t // tm),
        in_specs=in_specs,
        out_specs=pl.BlockSpec((tm, tn), lambda j, i: (i, j)),
        out_shape=jax.ShapeDtypeStruct((t, n), out_dtype),
        scratch_shapes=scratch,
        compiler_params=_params(2),
        name=name,
    )(*operands)


def _causal_conv3(u, halo, w_ref):
    rows, pad = u.shape[0], halo.shape[0]
    ext = jnp.concatenate([halo, u], axis=0)
    u1 = pltpu.roll(ext, 1, 0)[pad:pad + rows]
    u2 = pltpu.roll(ext, 2, 0)[pad:pad + rows]
    return w_ref[0:1, :] * u2 + w_ref[1:2, :] * u1 + w_ref[2:3, :] * u


def _conv_a_kernel(a_ref, win_ref, wb_ref, wc_ref, cw_ref, o_ref, winbf_ref, wbbf_ref, wcbf_ref,
                   halo_ref, *, tiles_per_seq):
    @pl.when(pl.program_id(1) % tiles_per_seq == 0)
    def _sequence_start():
        halo_ref[...] = jnp.zeros(halo_ref.shape, F32)

    def emit(a_in, a_b, a_c):
        u = a_c * a_in
        y = a_b * _causal_conv3(u, halo_ref[...], cw_ref)
        halo_ref[...] = u[u.shape[0] - halo_ref.shape[0]:, :]
        o_ref[...] = y.astype(o_ref.dtype)
    _with_rounded_weights([(a_ref, win_ref, winbf_ref), (a_ref, wb_ref, wbbf_ref),
                           (a_ref, wc_ref, wcbf_ref)], emit)


def _conv_a(h, w, layer, conv_w, *, seq, d_conv, tile=CONV_A_TILE):
    tm, tn = tile
    t, k = h.shape
    ncb = d_conv // tn
    assert seq % tm == 0
    section = lambda s: pl.BlockSpec((None, k, tn), lambda j, i: (layer, 0, s * ncb + j))
    return pl.pallas_call(
        functools.partial(_conv_a_kernel, tiles_per_seq=seq // tm),
        grid=(ncb, t // tm),
        in_specs=[pl.BlockSpec((tm, k), lambda j, i: (i, 0)), section(0), section(1), section(2),
                  pl.BlockSpec((CONV_WIDTH, tn), lambda j, i: (0, j))],
        out_specs=pl.BlockSpec((tm, tn), lambda j, i: (i, j)),
        out_shape=jax.ShapeDtypeStruct((t, d_conv), BF16),
        scratch_shapes=[pltpu.VMEM((k, tn), BF16)] * 3 + [pltpu.VMEM((F32_SUBLANES, tn), F32)],
        compiler_params=_params(2),
        name="conv_a",
    )(h, w, w, w, conv_w)


def _attn_kernel(slopes_ref, lq1_ref, lk1_ref, lq2_ref, lk2_ref, g_ref, q_ref, k_ref, v_ref,
                 o_ref, vt_ref, bias_ref, u_ref, p_ref, alpha_ref, m_ref, acc_ref,
                 *, lam_init, blk, seq):
    h = pl.program_id(1)
    slope2 = slopes_ref[h] * LOG2E
    nq = seq // blk
    assert nq >= 2

    def xpose(c, carry):
        rows = pl.ds(pl.multiple_of(c * blk, blk), blk)
        vt_ref[0:HEAD_DIM_V, rows] = v_ref[rows, :].T
        return carry
    lax.fori_loop(0, nq, xpose, 0)
    pad_row = lax.broadcasted_iota(jnp.int32, (BF16_SUBLANES, seq), 0)
    vt_ref[HEAD_DIM_V:, :] = jnp.where(pad_row == 0, 1.0, 0.0).astype(vt_ref.dtype)
    krel = lax.broadcasted_iota(jnp.int32, (blk, blk), 0)
    qrel = lax.broadcasted_iota(jnp.int32, (blk, blk), 1)
    bias_ref[0] = slope2 * krel.astype(F32)
    visible = (krel // CHUNK) <= (qrel // CHUNK)
    bias_ref[1] = jnp.where(visible, slope2 * (qrel - jnp.abs(qrel - krel)).astype(F32), NEG_INF)

    def rows_of(j):
        if isinstance(j, int):
            return pl.ds(j * blk, blk)
        return pl.ds(pl.multiple_of(j * blk, blk), blk)

    def max_row_sq(ref):
        def body(c, best):
            x = ref[rows_of(c), :].astype(F32)
            sq = (x * x).reshape(blk // F32_SUBLANES, F32_SUBLANES, x.shape[1])
            return jnp.maximum(best, jnp.max(sq, axis=0))
        folded = lax.fori_loop(0, nq, body, jnp.zeros((F32_SUBLANES, 2 * HEAD_DIM_QK), F32))
        per_map = jnp.maximum(jnp.sum(folded[:, :HEAD_DIM_QK], axis=1, keepdims=True),
                              jnp.sum(folded[:, HEAD_DIM_QK:], axis=1, keepdims=True))
        return jnp.max(per_map, axis=0, keepdims=True)

    logit_bound = jnp.sqrt(max_row_sq(q_ref)) * jnp.sqrt(max_row_sq(k_ref))
    far_keys = (2.0 * logit_bound + UNDERFLOW_LOG2) / slope2
    reach = jnp.max(jnp.minimum(jnp.floor(far_keys * (1.0 / blk)), float(nq))).astype(jnp.int32) + 1
    reach = jnp.minimum(reach, nq)
    n_items = (reach + 1) * (reach + 2) // 2 + (nq - 1 - reach) * (reach + 1)
    n_items = jnp.where(reach >= nq - 1, nq * (nq + 1) // 2, n_items)

    lam = (jnp.exp(jnp.sum(lq1_ref[...] * lk1_ref[...], axis=-1, keepdims=True))
           - jnp.exp(jnp.sum(lq2_ref[...] * lk2_ref[...], axis=-1, keepdims=True))
           + lam_init)
    out_gain = g_ref[...] * (1.0 - lam_init)

    def first_block(qi):
        return jnp.maximum(qi - reach, 0)

    def next_item(item):
        qi, j = item
        last = j == qi
        return jnp.where(last, qi + 1, qi), jnp.where(last, first_block(qi + 1), j + 1)

    def scores(item, slot):
        qi, j = item
        k = k_ref[rows_of(j), :]
        q = q_ref[rows_of(qi), :]
        which = jnp.asarray(j == qi, jnp.int32)
        for c in range(2):
            kc = k[:, c * HEAD_DIM_QK:(c + 1) * HEAD_DIM_QK]
            qc = q[:, c * HEAD_DIM_QK:(c + 1) * HEAD_DIM_QK]
            u = lax.dot_general(kc, qc, (((1,), (1,)), ((), ())), preferred_element_type=F32)
            u_ref[slot, c] = u + bias_ref[which]

    def softmax(item, slot):
        qi, j = item
        off = slope2 * jnp.full((1, blk), (j - qi) * blk, jnp.int32).astype(F32)
        first = jnp.full((1, blk), j - first_block(qi), jnp.int32) == 0
        for c in range(2):
            m_old = jnp.where(first, NEG_INF, m_ref[c])
            m_new = jnp.maximum(m_old, jnp.max(u_ref[slot, c], axis=0, keepdims=True) + off)
            p_ref[slot, c] = jnp.exp2(u_ref[slot, c] - (m_new - off)).astype(p_ref.dtype)
            alpha_ref[slot, c] = jnp.exp2(m_old - m_new)
            m_ref[c] = m_new

    def values(item, slot):
        qi, j = item
        vt = vt_ref[:, rows_of(j)]
        half = qi % 2
        for c in range(2):
            pv = jnp.dot(vt, p_ref[slot, c], preferred_element_type=F32)
            acc_ref[half, c] = alpha_ref[slot, c] * acc_ref[half, c] + pv

    def finish_query_block(qi):
        half = qi % 2
        l0 = acc_ref[half, 0, HEAD_DIM_V:HEAD_DIM_V + 1, :]
        l1 = acc_ref[half, 1, HEAD_DIM_V:HEAD_DIM_V + 1, :]
        ot = (acc_ref[half, 0, 0:HEAD_DIM_V, :] * (1.0 / l0)
              - acc_ref[half, 1, 0:HEAD_DIM_V, :] * (lam / l1))
        ot = ot * lax.rsqrt(jnp.mean(ot * ot, axis=0, keepdims=True) + EPS)
        o_ref[rows_of(qi), :] = (ot.T * out_gain).astype(o_ref.dtype)

    def is_last(item):
        return item[1] == item[0]

    def trip(items, slot):
        done, mid, ahead = items
        softmax(mid, 1 - slot)
        scores(ahead, slot)
        values(done, slot)
        return mid, ahead, next_item(ahead)

    def finish_if_last(*done_items):
        ended = functools.reduce(jnp.logical_or, [is_last(d) for d in done_items])
        which = done_items[-1][0]
        for d in reversed(done_items[:-1]):
            which = jnp.where(is_last(d), d[0], which)

        @pl.when(ended)
        def _():
            finish_query_block(which)

    def trip_pair(i, items):
        after_first = trip(items, 0)
        after_second = trip(after_first, 1)
        finish_if_last(items[0], after_first[0])
        return after_second

    acc_ref[...] = jnp.zeros(acc_ref.shape, F32)
    m_ref[...] = jnp.full(m_ref.shape, NEG_INF, F32)
    zero = jnp.int32(0)
    item0, item1 = (zero, zero), (zero + 1, zero)
    scores(item0, 0)
    scores(item1, 1)
    softmax(item0, 0)
    n_trips = n_items - 2
    items = lax.fori_loop(0, n_trips // 2, trip_pair, (item0, item1, next_item(item1)))

    @pl.when(n_trips % 2 == 1)
    def _():
        trip(items, 0)
        finish_if_last(items[0])

    last_slot = (n_items - 1) % 2
    values((nq - 1, nq - 2), 1 - last_slot)
    softmax((nq - 1, nq - 1), last_slot)
    values((nq - 1, nq - 1), last_slot)
    finish_query_block(nq - 1)


def _diff_attention(proj, lam_q1, lam_k1, lam_q2, lam_k2, subln_g, *, batch, seq, q_col, k_col,
                    v_col, lam_init, blk=ATTN_BLOCK):
    t = proj.shape[0]
    hw = 2 * HEAD_DIM_QK
    assert hw == HEAD_DIM_V and seq % blk == 0 and blk % CHUNK == 0
    vt_rows = HEAD_DIM_V + BF16_SUBLANES
    slopes = 2.0 ** (-8.0 * jnp.arange(1, N_HEADS + 1, dtype=F32) / N_HEADS)
    vec = lambda n: pl.BlockSpec((1, n), lambda b, h, s: (0, 0))
    head_cols = lambda col: pl.BlockSpec((seq, hw), lambda b, h, s: (b, col // hw + h))
    grid_spec = pltpu.PrefetchScalarGridSpec(
        num_scalar_prefetch=1,
        grid=(batch, N_HEADS),
        in_specs=[vec(HEAD_DIM_QK), vec(HEAD_DIM_QK), vec(HEAD_DIM_QK), vec(HEAD_DIM_QK),
                  vec(HEAD_DIM_V), head_cols(q_col), head_cols(k_col), head_cols(v_col)],
        out_specs=pl.BlockSpec((seq, HEAD_DIM_V), lambda b, h, s: (b, h)),
        scratch_shapes=[pltpu.VMEM((vt_rows, seq), BF16),
                        pltpu.VMEM((2, blk, blk), F32),
                        pltpu.VMEM((2, 2, blk, blk), F32),
                        pltpu.VMEM((2, 2, blk, blk), BF16),
                        pltpu.VMEM((2, 2, 1, blk), F32),
                        pltpu.VMEM((2, 1, blk), F32),
                        pltpu.VMEM((2, 2, vt_rows, blk), F32)],
    )
    return pl.pallas_call(
        functools.partial(_attn_kernel, lam_init=lam_init, blk=blk, seq=seq),
        grid_spec=grid_spec,
        out_shape=jax.ShapeDtypeStruct((t, N_HEADS * HEAD_DIM_V), BF16),
        compiler_params=_params(2),
        name="diff_attn",
    )(slopes, lam_q1.reshape(1, -1), lam_k1.reshape(1, -1), lam_q2.reshape(1, -1),
      lam_k2.reshape(1, -1), subln_g.reshape(1, -1), proj, proj, proj)


def _merge_kernel(ca_ref, at_ref, wa_ref, wb_ref, ga_ref, gb_ref, ba_ref, bb_ref, o_ref,
                  wabf_ref, wbbf_ref):
    def emit(y_a, y_b):
        g_a = jax.nn.sigmoid(ga_ref[...].astype(F32) + ba_ref[...])
        g_b = jax.nn.sigmoid(gb_ref[...].astype(F32) + bb_ref[...])
        o_ref[...] = (g_a * y_a + g_b * y_b).astype(o_ref.dtype)
    _with_rounded_weights([(ca_ref, wa_ref, wabf_ref), (at_ref, wb_ref, wbbf_ref)], emit)


def _merge(ca, attn, w_a, w_b, layer, proj, b_gate, *, gate_col, d_model, tile=MERGE_TILE):
    tm, tn = tile
    t, ka = ca.shape
    kb = attn.shape[1]
    gcb = gate_col // tn
    ncb = d_model // tn
    bias = b_gate.reshape(1, -1)
    return pl.pallas_call(
        _merge_kernel,
        grid=(ncb, t // tm),
        in_specs=[pl.BlockSpec((tm, ka), lambda j, i: (i, 0)),
                  pl.BlockSpec((tm, kb), lambda j, i: (i, 0)),
                  pl.BlockSpec((None, ka, tn), lambda j, i: (layer, 0, j)),
                  pl.BlockSpec((None, kb, tn), lambda j, i: (layer, 0, j)),
                  pl.BlockSpec((tm, tn), lambda j, i: (i, gcb + j)),
                  pl.BlockSpec((tm, tn), lambda j, i: (i, gcb + ncb + j)),
                  pl.BlockSpec((1, tn), lambda j, i: (0, j)),
                  pl.BlockSpec((1, tn), lambda j, i: (0, ncb + j))],
        out_specs=pl.BlockSpec((tm, tn), lambda j, i: (i, j)),
        out_shape=jax.ShapeDtypeStruct((t, d_model), BF16),
        scratch_shapes=[pltpu.VMEM((ka, tn), BF16), pltpu.VMEM((kb, tn), BF16)],
        compiler_params=_params(2),
        name="merge",
    )(ca, attn, w_a, w_b, proj, proj, bias, bias)


def _ffn_in_kernel(a_ref, wg_ref, wu_ref, cw_ref, o_ref, wgbf_ref, wubf_ref, halo_ref,
                   *, tiles_per_seq):
    @pl.when(pl.program_id(1) % tiles_per_seq == 0)
    def _sequence_start():
        halo_ref[...] = jnp.zeros(halo_ref.shape, F32)

    def emit(gate, up):
        z = _causal_conv3(gate, halo_ref[...], cw_ref)
        halo_ref[...] = gate[gate.shape[0] - halo_ref.shape[0]:, :]
        c0 = math.sqrt(2.0 / math.pi)
        act = 0.5 * z * (1.0 + jnp.tanh(c0 * (z + 0.044715 * (z * z * z))))
        o_ref[...] = (act * up).astype(o_ref.dtype)
    _with_rounded_weights([(a_ref, wg_ref, wgbf_ref), (a_ref, wu_ref, wubf_ref)], emit)


def _ffn_in(h, w, layer, conv_w, *, seq, d_ff, tile=FFN_IN_TILE):
    tm, tn = tile
    t, k = h.shape
    ncb = d_ff // tn
    assert seq % tm == 0
    return pl.pallas_call(
        functools.partial(_ffn_in_kernel, tiles_per_seq=seq // tm),
        grid=(ncb, t // tm),
        in_specs=[pl.BlockSpec((tm, k), lambda j, i: (i, 0)),
                  pl.BlockSpec((None, k, tn), lambda j, i: (layer, 0, j)),
                  pl.BlockSpec((None, k, tn), lambda j, i: (layer, 0, ncb + j)),
                  pl.BlockSpec((CONV_WIDTH, tn), lambda j, i: (0, j))],
        out_specs=pl.BlockSpec((tm, tn), lambda j, i: (i, j)),
        out_shape=jax.ShapeDtypeStruct((t, d_ff), BF16),
        scratch_shapes=[pltpu.VMEM((k, tn), BF16), pltpu.VMEM((k, tn), BF16),
                        pltpu.VMEM((F32_SUBLANES, tn), F32)],
        compiler_params=_params(2),
        name="ffn_in",
    )(h, w, w, conv_w)


def _lambda_init(layer_idx):
    return 0.8 - 0.6 * math.exp(-0.3 * layer_idx)


def kernel(x, w_in, b_gate, conv_a, w_a_out, lam_q1, lam_k1, lam_q2, lam_k2, subln_g, w_b_out, w_o,
           norm_mix_pre, norm_mix_post, w_ffn_in, conv_ffn, w_ffn_out, norm_ffn_pre, norm_ffn_post):
    batch, seq, d_model = x.shape
    depth = w_in.shape[0]
    d_conv = conv_a.shape[-1]
    d_ff = conv_ffn.shape[-1]
    d_qk = N_HEADS * 2 * HEAD_DIM_QK
    proj_start = 3 * d_conv
    q_col = 0
    k_col = q_col + d_qk
    v_col = k_col + d_qk
    gate_col = v_col + N_HEADS * HEAD_DIM_V

    d_in = w_in.shape[-1]
    q_scale = jnp.ones((d_in,), F32).at[proj_start + q_col:proj_start + k_col].set(QK_SCALE_LOG2)

    w_ffn_out_bf = w_ffn_out.astype(BF16)

    xt = x.reshape(batch * seq, d_model)
    h = _norm(xt, norm_mix_pre[0])
    for l in range(depth):
        ca = _conv_a(h, w_in, l, conv_a[l], seq=seq, d_conv=d_conv)
        proj = _matmul(h, w_in, l, BF16, tile=IN_PROJ_TILE, name="in_proj", col_scale=q_scale,
                       col_start=proj_start)
        attn = _diff_attention(proj, lam_q1[l], lam_k1[l], lam_q2[l], lam_k2[l], subln_g[l],
                               batch=batch, seq=seq, q_col=q_col, k_col=k_col, v_col=v_col,
                               lam_init=_lambda_init(l))
        merged = _merge(ca, attn, w_a_out, w_b_out, l, proj, b_gate[l],
                        gate_col=gate_col, d_model=d_model)
        mix = _matmul(merged, w_o, l, BF16, tile=W_O_TILE, name="w_o")
        xt, h2 = _resid_norm(xt, mix, norm_mix_post[l], norm_ffn_pre[l])

        f = _ffn_in(h2, w_ffn_in, l, conv_ffn[l], seq=seq, d_ff=d_ff)
        ffn = _matmul(f, w_ffn_out_bf, l, BF16, tile=FFN_OUT_TILE, name="ffn_out")
        g_next = norm_mix_pre[l + 1] if l + 1 < depth else None
        xt, h = _resid_norm(xt, ffn, norm_ffn_post[l], g_next)
    return xt.reshape(batch, seq, d_model)
```

```python
import functools
import math

import jax
import jax.numpy as jnp
from jax import lax
from jax.experimental import pallas as pl
from jax.experimental.pallas import tpu as pltpu

CHUNK = 64
CONV_WIDTH = 3
N_HEADS = 8
HEAD_DIM_QK = 128
HEAD_DIM_V = 256
EPS = 1e-6
NEG_INF = -1e30
UNDERFLOW_LOG2 = 150.0
LOG2E = math.log2(math.e)
QK_SCALE_LOG2 = HEAD_DIM_QK ** -0.5 * LOG2E

VMEM_LIMIT_BYTES = 56 * 1024 * 1024
BF16_SUBLANES = 16
F32_SUBLANES = 8
W_CAST_CHUNKS = 4

NORM_ROWS = 256
CONV_A_TILE = (1024, 256)
IN_PROJ_TILE = (512, 1024)
MERGE_TILE = (1024, 512)
W_O_TILE = (512, 1024)
FFN_IN_TILE = (1024, 256)
FFN_OUT_TILE = (512, 512)
ATTN_BLOCK = 512

F32 = jnp.float32
BF16 = jnp.bfloat16


def _params(n_grid_dims):
    return pltpu.CompilerParams(
        dimension_semantics=("arbitrary",) * n_grid_dims,
        vmem_limit_bytes=VMEM_LIMIT_BYTES)


def _rms(x, g):
    return x * lax.rsqrt(jnp.mean(x * x, axis=-1, keepdims=True) + EPS) * g


def _norm_kernel(x_ref, g_ref, h_ref):
    h_ref[...] = _rms(x_ref[...], g_ref[...]).astype(h_ref.dtype)


def _norm(x, g, *, rows=NORM_ROWS):
    t, d = x.shape
    return pl.pallas_call(
        _norm_kernel,
        grid=(t // rows,),
        in_specs=[pl.BlockSpec((rows, d), lambda i: (i, 0)),
                  pl.BlockSpec((1, d), lambda i: (0, 0))],
        out_specs=pl.BlockSpec((rows, d), lambda i: (i, 0)),
        out_shape=jax.ShapeDtypeStruct((t, d), BF16),
        compiler_params=_params(1),
        name="norm",
    )(x, g.reshape(1, d))


def _resid_norm_kernel(x_ref, y_ref, gpost_ref, gpre_ref, xo_ref, h_ref):
    x_new = x_ref[...] + _rms(y_ref[...].astype(F32), gpost_ref[...])
    xo_ref[...] = x_new
    h_ref[...] = _rms(x_new, gpre_ref[...]).astype(h_ref.dtype)


def _resid_kernel(x_ref, y_ref, gpost_ref, xo_ref):
    xo_ref[...] = x_ref[...] + _rms(y_ref[...].astype(F32), gpost_ref[...])


def _resid_norm(x, y, g_post, g_pre_next, *, rows=NORM_ROWS):
    t, d = x.shape
    row_spec = pl.BlockSpec((rows, d), lambda i: (i, 0))
    vec_spec = pl.BlockSpec((1, d), lambda i: (0, 0))
    if g_pre_next is None:
        return pl.pallas_call(
            _resid_kernel,
            grid=(t // rows,),
            in_specs=[row_spec, row_spec, vec_spec],
            out_specs=row_spec,
            out_shape=jax.ShapeDtypeStruct((t, d), F32),
            compiler_params=_params(1),
            name="resid",
        )(x, y, g_post.reshape(1, d)), None
    return pl.pallas_call(
        _resid_norm_kernel,
        grid=(t // rows,),
        in_specs=[row_spec, row_spec, vec_spec, vec_spec],
        out_specs=[row_spec, row_spec],
        out_shape=[jax.ShapeDtypeStruct((t, d), F32), jax.ShapeDtypeStruct((t, d), BF16)],
        compiler_params=_params(1),
        name="resid_norm",
    )(x, y, g_post.reshape(1, d), g_pre_next.reshape(1, d))


def _dot_rounding_in_chunks(a_ref, w_ref, wbf_ref):
    kc = w_ref.shape[0] // W_CAST_CHUNKS
    acc = None
    for c in range(W_CAST_CHUNKS):
        ks = pl.ds(c * kc, kc)
        wbf_ref[ks, :] = w_ref[ks, :].astype(wbf_ref.dtype)
        part = jnp.dot(a_ref[:, ks], wbf_ref[ks, :], preferred_element_type=F32)
        acc = part if acc is None else acc + part
    return acc


def _with_rounded_weights(operands, emit):
    @pl.when(pl.program_id(1) == 0)
    def _first_token_tile():
        emit(*[_dot_rounding_in_chunks(*op) for op in operands])

    @pl.when(pl.program_id(1) != 0)
    def _later_token_tiles():
        emit(*[jnp.dot(a_ref[...], wbf_ref[...], preferred_element_type=F32)
               for a_ref, _, wbf_ref in operands])


def _mm_direct_kernel(a_ref, w_ref, o_ref):
    o_ref[...] = jnp.dot(a_ref[...], w_ref[...], preferred_element_type=F32).astype(o_ref.dtype)


def _mm_kernel(a_ref, w_ref, o_ref, wbf_ref):
    def emit(acc):
        o_ref[...] = acc.astype(o_ref.dtype)
    _with_rounded_weights([(a_ref, w_ref, wbf_ref)], emit)


def _mm_colscale_kernel(a_ref, w_ref, s_ref, o_ref, wbf_ref):
    def emit(acc):
        o_ref[...] = (acc * s_ref[...]).astype(o_ref.dtype)
    _with_rounded_weights([(a_ref, w_ref, wbf_ref)], emit)


def _matmul(a, w, layer, out_dtype, *, tile, name, col_scale=None, col_start=0):
    tm, tn = tile
    t, k = a.shape
    n = w.shape[2] - col_start
    c0 = col_start // tn
    assert col_start % tn == 0
    in_specs = [pl.BlockSpec((tm, k), lambda j, i: (i, 0)),
                pl.BlockSpec((None, k, tn), lambda j, i: (layer, 0, c0 + j))]
    operands = [a, w]
    if col_scale is not None:
        in_specs.append(pl.BlockSpec((1, tn), lambda j, i: (0, c0 + j)))
        operands.append(col_scale.reshape(1, -1))
    if w.dtype == BF16:
        assert col_scale is None
        body, scratch = _mm_direct_kernel, []
    else:
        body = _mm_kernel if col_scale is None else _mm_colscale_kernel
        scratch = [pltpu.VMEM((k, tn), BF16)]
    return pl.pallas_call(
        body,
        grid=(n // tn, t // tm),
        in_specs=in_specs,
        out_specs=pl.BlockSpec((tm, tn), lambda j, i: (i, j)),
        out_shape=jax.ShapeDtypeStruct((t, n), out_dtype),
        scratch_shapes=scratch,
        compiler_params=_params(2),
        name=name,
    )(*operands)


def _causal_conv3(u, halo, w_ref):
    rows, pad = u.shape[0], halo.shape[0]
    ext = jnp.concatenate([halo, u], axis=0)
    u1 = pltpu.roll(ext, 1, 0)[pad:pad + rows]
    u2 = pltpu.roll(ext, 2, 0)[pad:pad + rows]
    return w_ref[0:1, :] * u2 + w_ref[1:2, :] * u1 + w_ref[2:3, :] * u


def _conv_a_kernel(a_ref, win_ref, wb_ref, wc_ref, cw_ref, o_ref, winbf_ref, wbbf_ref, wcbf_ref,
                   halo_ref, *, tiles_per_seq):
    @pl.when(pl.program_id(1) % tiles_per_seq == 0)
    def _sequence_start():
        halo_ref[...] = jnp.zeros(halo_ref.shape, F32)

    def emit(a_in, a_b, a_c):
        u = a_c * a_in
        y = a_b * _causal_conv3(u, halo_ref[...], cw_ref)
        halo_ref[...] = u[u.shape[0] - halo_ref.shape[0]:, :]
        o_ref[...] = y.astype(o_ref.dtype)
    _with_rounded_weights([(a_ref, win_ref, winbf_ref), (a_ref, wb_ref, wbbf_ref),
                           (a_ref, wc_ref, wcbf_ref)], emit)


def _conv_a(h, w, layer, conv_w, *, seq, d_conv, tile=CONV_A_TILE):
    tm, tn = tile
    t, k = h.shape
    ncb = d_conv // tn
    assert seq % tm == 0
    section = lambda s: pl.BlockSpec((None, k, tn), lambda j, i: (layer, 0, s * ncb + j))
    return pl.pallas_call(
        functools.partial(_conv_a_kernel, tiles_per_seq=seq // tm),
        grid=(ncb, t // tm),
        in_specs=[pl.BlockSpec((tm, k), lambda j, i: (i, 0)), section(0), section(1), section(2),
                  pl.BlockSpec((CONV_WIDTH, tn), lambda j, i: (0, j))],
        out_specs=pl.BlockSpec((tm, tn), lambda j, i: (i, j)),
        out_shape=jax.ShapeDtypeStruct((t, d_conv), BF16),
        scratch_shapes=[pltpu.VMEM((k, tn), BF16)] * 3 + [pltpu.VMEM((F32_SUBLANES, tn), F32)],
        compiler_params=_params(2),
        name="conv_a",
    )(h, w, w, w, conv_w)


def _attn_kernel(slopes_ref, lq1_ref, lk1_ref, lq2_ref, lk2_ref, g_ref, q_ref, k_ref, v_ref,
                 o_ref, vt_ref, bias_ref, u_ref, p_ref, alpha_ref, m_ref, acc_ref,
                 *, lam_init, blk, seq):
    h = pl.program_id(1)
    slope2 = slopes_ref[h] * LOG2E
    nq = seq // blk
    assert nq >= 2

    def rows_of(j):
        if isinstance(j, int):
            return pl.ds(j * blk, blk)
        return pl.ds(pl.multiple_of(j * blk, blk), blk)

    def fold_sq(ref, rows, best):
        x = ref[rows, :].astype(F32)
        sq = (x * x).reshape(blk // F32_SUBLANES, F32_SUBLANES, x.shape[1])
        return jnp.maximum(best, jnp.max(sq, axis=0))

    def per_chunk(c, carry):
        rows = rows_of(c)
        vt_ref[0:HEAD_DIM_V, rows] = v_ref[rows, :].T
        return fold_sq(q_ref, rows, carry[0]), fold_sq(k_ref, rows, carry[1])
    no_rows = jnp.zeros((F32_SUBLANES, 2 * HEAD_DIM_QK), F32)
    q_folded, k_folded = lax.fori_loop(0, nq, per_chunk, (no_rows, no_rows))
    pad_row = lax.broadcasted_iota(jnp.int32, (BF16_SUBLANES, seq), 0)
    vt_ref[HEAD_DIM_V:, :] = jnp.where(pad_row == 0, 1.0, 0.0).astype(vt_ref.dtype)
    krel = lax.broadcasted_iota(jnp.int32, (blk, blk), 0)
    qrel = lax.broadcasted_iota(jnp.int32, (blk, blk), 1)
    bias_ref[0] = slope2 * krel.astype(F32)
    visible = (krel // CHUNK) <= (qrel // CHUNK)
    bias_ref[1] = jnp.where(visible, slope2 * (qrel - jnp.abs(qrel - krel)).astype(F32), NEG_INF)

    def max_row_sq(folded):
        per_map = jnp.maximum(jnp.sum(folded[:, :HEAD_DIM_QK], axis=1, keepdims=True),
                              jnp.sum(folded[:, HEAD_DIM_QK:], axis=1, keepdims=True))
        return jnp.max(per_map, axis=0, keepdims=True)

    logit_bound = jnp.sqrt(max_row_sq(q_folded)) * jnp.sqrt(max_row_sq(k_folded))
    far_keys = (2.0 * logit_bound + UNDERFLOW_LOG2) / slope2
    reach = jnp.max(jnp.minimum(jnp.floor(far_keys * (1.0 / blk)), float(nq))).astype(jnp.int32) + 1
    reach = jnp.minimum(reach, nq)
    n_items = (reach + 1) * (reach + 2) // 2 + (nq - 1 - reach) * (reach + 1)
    n_items = jnp.where(reach >= nq - 1, nq * (nq + 1) // 2, n_items)

    lam = (jnp.exp(jnp.sum(lq1_ref[...] * lk1_ref[...], axis=-1, keepdims=True))
           - jnp.exp(jnp.sum(lq2_ref[...] * lk2_ref[...], axis=-1, keepdims=True))
           + lam_init)
    out_gain = (g_ref[...] * (1.0 - lam_init)).T

    def first_block(qi):
        return jnp.maximum(qi - reach, 0)

    def next_item(item):
        qi, j = item
        last = j == qi
        return jnp.where(last, qi + 1, qi), jnp.where(last, first_block(qi + 1), j + 1)

    def scores(item, slot):
        qi, j = item
        k = k_ref[rows_of(j), :]
        q = q_ref[rows_of(qi), :]
        which = jnp.asarray(j == qi, jnp.int32)
        for c in range(2):
            kc = k[:, c * HEAD_DIM_QK:(c + 1) * HEAD_DIM_QK]
            qc = q[:, c * HEAD_DIM_QK:(c + 1) * HEAD_DIM_QK]
            u = lax.dot_general(kc, qc, (((1,), (1,)), ((), ())), preferred_element_type=F32)
            u_ref[slot, c] = u + bias_ref[which]

    def softmax(item, slot):
        qi, j = item
        off = slope2 * jnp.full((1, blk), (j - qi) * blk, jnp.int32).astype(F32)
        first = jnp.full((1, blk), j - first_block(qi), jnp.int32) == 0
        for c in range(2):
            m_old = jnp.where(first, NEG_INF, m_ref[c])
            m_new = jnp.maximum(m_old, jnp.max(u_ref[slot, c], axis=0, keepdims=True) + off)
            p_ref[slot, c] = jnp.exp2(u_ref[slot, c] - (m_new - off)).astype(p_ref.dtype)
            alpha_ref[slot, c] = jnp.exp2(m_old - m_new)
            m_ref[c] = m_new

    def values(item, slot):
        qi, j = item
        vt = vt_ref[:, rows_of(j)]
        half = qi % 2
        for c in range(2):
            pv = jnp.dot(vt, p_ref[slot, c], preferred_element_type=F32)
            acc_ref[half, c] = alpha_ref[slot, c] * acc_ref[half, c] + pv

    def finish_query_block(qi):
        half = qi % 2
        l0 = acc_ref[half, 0, HEAD_DIM_V:HEAD_DIM_V + 1, :]
        l1 = acc_ref[half, 1, HEAD_DIM_V:HEAD_DIM_V + 1, :]
        ot = (acc_ref[half, 0, 0:HEAD_DIM_V, :] * (1.0 / l0)
              - acc_ref[half, 1, 0:HEAD_DIM_V, :] * (lam / l1))
        ot = ot * lax.rsqrt(jnp.mean(ot * ot, axis=0, keepdims=True) + EPS)
        o_ref[rows_of(qi), :] = (ot * out_gain).astype(o_ref.dtype).T

    def is_last(item):
        return item[1] == item[0]

    def trip(items, slot):
        done, mid, ahead = items
        softmax(mid, 1 - slot)
        scores(ahead, slot)
        values(done, slot)
        return mid, ahead, next_item(ahead)

    def finish_if_last(*done_items):
        ended = functools.reduce(jnp.logical_or, [is_last(d) for d in done_items])
        which = done_items[-1][0]
        for d in reversed(done_items[:-1]):
            which = jnp.where(is_last(d), d[0], which)

        @pl.when(ended)
        def _():
            finish_query_block(which)

    def trip_pair(i, items):
        after_first = trip(items, 0)
        after_second = trip(after_first, 1)
        finish_if_last(items[0], after_first[0])
        return after_second

    acc_ref[...] = jnp.zeros(acc_ref.shape, F32)
    m_ref[...] = jnp.full(m_ref.shape, NEG_INF, F32)
    zero = jnp.int32(0)
    item0, item1 = (zero, zero), (zero + 1, zero)
    scores(item0, 0)
    scores(item1, 1)
    softmax(item0, 0)
    n_trips = n_items - 2
    items = lax.fori_loop(0, n_trips // 2, trip_pair, (item0, item1, next_item(item1)))

    @pl.when(n_trips % 2 == 1)
    def _():
        trip(items, 0)
        finish_if_last(items[0])

    last_slot = (n_items - 1) % 2
    values((nq - 1, nq - 2), 1 - last_slot)
    softmax((nq - 1, nq - 1), last_slot)
    values((nq - 1, nq - 1), last_slot)
    finish_query_block(nq - 1)


def _diff_attention(proj, lam_q1, lam_k1, lam_q2, lam_k2, subln_g, *, batch, seq, q_col, k_col,
                    v_col, lam_init, blk=ATTN_BLOCK):
    t = proj.shape[0]
    hw = 2 * HEAD_DIM_QK
    assert hw == HEAD_DIM_V and seq % blk == 0 and blk % CHUNK == 0
    vt_rows = HEAD_DIM_V + BF16_SUBLANES
    slopes = 2.0 ** (-8.0 * jnp.arange(1, N_HEADS + 1, dtype=F32) / N_HEADS)
    vec = lambda n: pl.BlockSpec((1, n), lambda b, h, s: (0, 0))
    head_cols = lambda col: pl.BlockSpec((seq, hw), lambda b, h, s: (b, col // hw + h))
    grid_spec = pltpu.PrefetchScalarGridSpec(
        num_scalar_prefetch=1,
        grid=(batch, N_HEADS),
        in_specs=[vec(HEAD_DIM_QK), vec(HEAD_DIM_QK), vec(HEAD_DIM_QK), vec(HEAD_DIM_QK),
                  vec(HEAD_DIM_V), head_cols(q_col), head_cols(k_col), head_cols(v_col)],
        out_specs=pl.BlockSpec((seq, HEAD_DIM_V), lambda b, h, s: (b, h)),
        scratch_shapes=[pltpu.VMEM((vt_rows, seq), BF16),
                        pltpu.VMEM((2, blk, blk), F32),
                        pltpu.VMEM((2, 2, blk, blk), F32),
                        pltpu.VMEM((2, 2, blk, blk), BF16),
                        pltpu.VMEM((2, 2, 1, blk), F32),
                        pltpu.VMEM((2, 1, blk), F32),
                        pltpu.VMEM((2, 2, vt_rows, blk), F32)],
    )
    return pl.pallas_call(
        functools.partial(_attn_kernel, lam_init=lam_init, blk=blk, seq=seq),
        grid_spec=grid_spec,
        out_shape=jax.ShapeDtypeStruct((t, N_HEADS * HEAD_DIM_V), BF16),
        compiler_params=_params(2),
        name="diff_attn",
    )(slopes, lam_q1.reshape(1, -1), lam_k1.reshape(1, -1), lam_q2.reshape(1, -1),
      lam_k2.reshape(1, -1), subln_g.reshape(1, -1), proj, proj, proj)


def _merge_kernel(ca_ref, at_ref, wa_ref, wb_ref, ga_ref, gb_ref, ba_ref, bb_ref, o_ref,
                  wabf_ref, wbbf_ref):
    def emit(y_a, y_b):
        g_a = jax.nn.sigmoid(ga_ref[...].astype(F32) + ba_ref[...])
        g_b = jax.nn.sigmoid(gb_ref[...].astype(F32) + bb_ref[...])
        o_ref[...] = (g_a * y_a + g_b * y_b).astype(o_ref.dtype)
    _with_rounded_weights([(ca_ref, wa_ref, wabf_ref), (at_ref, wb_ref, wbbf_ref)], emit)


def _merge(ca, attn, w_a, w_b, layer, proj, b_gate, *, gate_col, d_model, tile=MERGE_TILE):
    tm, tn = tile
    t, ka = ca.shape
    kb = attn.shape[1]
    gcb = gate_col // tn
    ncb = d_model // tn
    bias = b_gate.reshape(1, -1)
    return pl.pallas_call(
        _merge_kernel,
        grid=(ncb, t // tm),
        in_specs=[pl.BlockSpec((tm, ka), lambda j, i: (i, 0)),
                  pl.BlockSpec((tm, kb), lambda j, i: (i, 0)),
                  pl.BlockSpec((None, ka, tn), lambda j, i: (layer, 0, j)),
                  pl.BlockSpec((None, kb, tn), lambda j, i: (layer, 0, j)),
                  pl.BlockSpec((tm, tn), lambda j, i: (i, gcb + j)),
                  pl.BlockSpec((tm, tn), lambda j, i: (i, gcb + ncb + j)),
                  pl.BlockSpec((1, tn), lambda j, i: (0, j)),
                  pl.BlockSpec((1, tn), lambda j, i: (0, ncb + j))],
        out_specs=pl.BlockSpec((tm, tn), lambda j, i: (i, j)),
        out_shape=jax.ShapeDtypeStruct((t, d_model), BF16),
        scratch_shapes=[pltpu.VMEM((ka, tn), BF16), pltpu.VMEM((kb, tn), BF16)],
        compiler_params=_params(2),
        name="merge",
    )(ca, attn, w_a, w_b, proj, proj, bias, bias)


def _ffn_in_kernel(a_ref, wg_ref, wu_ref, cw_ref, o_ref, wgbf_ref, wubf_ref, halo_ref,
                   *, tiles_per_seq):
    @pl.when(pl.program_id(1) % tiles_per_seq == 0)
    def _sequence_start():
        halo_ref[...] = jnp.zeros(halo_ref.shape, F32)

    def emit(gate, up):
        z = _causal_conv3(gate, halo_ref[...], cw_ref)
        halo_ref[...] = gate[gate.shape[0] - halo_ref.shape[0]:, :]
        c0 = math.sqrt(2.0 / math.pi)
        act = 0.5 * z * (1.0 + jnp.tanh(c0 * (z + 0.044715 * (z * z * z))))
        o_ref[...] = (act * up).astype(o_ref.dtype)
    _with_rounded_weights([(a_ref, wg_ref, wgbf_ref), (a_ref, wu_ref, wubf_ref)], emit)


def _ffn_in(h, w, layer, conv_w, *, seq, d_ff, tile=FFN_IN_TILE):
    tm, tn = tile
    t, k = h.shape
    ncb = d_ff // tn
    assert seq % tm == 0
    return pl.pallas_call(
        functools.partial(_ffn_in_kernel, tiles_per_seq=seq // tm),
        grid=(ncb, t // tm),
        in_specs=[pl.BlockSpec((tm, k), lambda j, i: (i, 0)),
                  pl.BlockSpec((None, k, tn), lambda j, i: (layer, 0, j)),
                  pl.BlockSpec((None, k, tn), lambda j, i: (layer, 0, ncb + j)),
                  pl.BlockSpec((CONV_WIDTH, tn), lambda j, i: (0, j))],
        out_specs=pl.BlockSpec((tm, tn), lambda j, i: (i, j)),
        out_shape=jax.ShapeDtypeStruct((t, d_ff), BF16),
        scratch_shapes=[pltpu.VMEM((k, tn), BF16), pltpu.VMEM((k, tn), BF16),
                        pltpu.VMEM((F32_SUBLANES, tn), F32)],
        compiler_params=_params(2),
        name="ffn_in",
    )(h, w, w, conv_w)


def _lambda_init(layer_idx):
    return 0.8 - 0.6 * math.exp(-0.3 * layer_idx)


def kernel(x, w_in, b_gate, conv_a, w_a_out, lam_q1, lam_k1, lam_q2, lam_k2, subln_g, w_b_out, w_o,
           norm_mix_pre, norm_mix_post, w_ffn_in, conv_ffn, w_ffn_out, norm_ffn_pre, norm_ffn_post):
    batch, seq, d_model = x.shape
    depth = w_in.shape[0]
    d_conv = conv_a.shape[-1]
    d_ff = conv_ffn.shape[-1]
    d_qk = N_HEADS * 2 * HEAD_DIM_QK
    proj_start = 3 * d_conv
    q_col = 0
    k_col = q_col + d_qk
    v_col = k_col + d_qk
    gate_col = v_col + N_HEADS * HEAD_DIM_V

    d_in = w_in.shape[-1]
    q_scale = jnp.ones((d_in,), F32).at[proj_start + q_col:proj_start + k_col].set(QK_SCALE_LOG2)

    w_ffn_out_bf = w_ffn_out.astype(BF16)

    xt = x.reshape(batch * seq, d_model)
    h = _norm(xt, norm_mix_pre[0])
    for l in range(depth):
        ca = _conv_a(h, w_in, l, conv_a[l], seq=seq, d_conv=d_conv)
        proj = _matmul(h, w_in, l, BF16, tile=IN_PROJ_TILE, name="in_proj", col_scale=q_scale,
                       col_start=proj_start)
        attn = _diff_attention(proj, lam_q1[l], lam_k1[l], lam_q2[l], lam_k2[l], subln_g[l],
                               batch=batch, seq=seq, q_col=q_col, k_col=k_col, v_col=v_col,
                               lam_init=_lambda_init(l))
        merged = _merge(ca, attn, w_a_out, w_b_out, l, proj, b_gate[l],
                        gate_col=gate_col, d_model=d_model)
        mix = _matmul(merged, w_o, l, BF16, tile=W_O_TILE, name="w_o")
        xt, h2 = _resid_norm(xt, mix, norm_mix_post[l], norm_ffn_pre[l])

        f = _ffn_in(h2, w_ffn_in, l, conv_ffn[l], seq=seq, d_ff=d_ff)
        ffn = _matmul(f, w_ffn_out_bf, l, BF16, tile=FFN_OUT_TILE, name="ffn_out")
        g_next = norm_mix_pre[l + 1] if l + 1 < depth else None
        xt, h = _resid_norm(xt, ffn, norm_ffn_post[l], g_next)
    return xt.reshape(batch, seq, d_model)
```

```python
import functools
import math

import jax
import jax.numpy as jnp
from jax import lax
from jax.experimental import pallas as pl
from jax.experimental.pallas import tpu as pltpu

CHUNK = 64
CONV_WIDTH = 3
N_HEADS = 8
HEAD_DIM_QK = 128
HEAD_DIM_V = 256
EPS = 1e-6
NEG_INF = -1e30
UNDERFLOW_LOG2 = 150.0
LOG2E = math.log2(math.e)
QK_SCALE_LOG2 = HEAD_DIM_QK ** -0.5 * LOG2E

VMEM_LIMIT_BYTES = 56 * 1024 * 1024
BF16_SUBLANES = 16
F32_SUBLANES = 8
NORM_FOLD = 8
W_CAST_CHUNKS = 4

NORM_ROWS = 256
CONV_A_TILE = (1024, 256)
IN_PROJ_TILE = (512, 1024)
MERGE_TILE = (1024, 512)
W_O_TILE = (512, 1024)
FFN_IN_TILE = (1024, 256)
FFN_OUT_TILE = (512, 512)
ATTN_BLOCK = 512

F32 = jnp.float32
BF16 = jnp.bfloat16


def _params(n_grid_dims):
    return pltpu.CompilerParams(
        dimension_semantics=("arbitrary",) * n_grid_dims,
        vmem_limit_bytes=VMEM_LIMIT_BYTES)


def _rms(x, g):
    return x * lax.rsqrt(jnp.mean(x * x, axis=-1, keepdims=True) + EPS) * g


def _norm_kernel(x_ref, g_ref, h_ref):
    h_ref[...] = _rms(x_ref[...], g_ref[...]).astype(h_ref.dtype)


def _norm(x, g, *, rows=NORM_ROWS):
    t, d = x.shape
    return pl.pallas_call(
        _norm_kernel,
        grid=(t // rows,),
        in_specs=[pl.BlockSpec((rows, d), lambda i: (i, 0)),
                  pl.BlockSpec((1, d), lambda i: (0, 0))],
        out_specs=pl.BlockSpec((rows, d), lambda i: (i, 0)),
        out_shape=jax.ShapeDtypeStruct((t, d), BF16),
        compiler_params=_params(1),
        name="norm",
    )(x, g.reshape(1, d))


def _resid_norm_kernel(x_ref, y_ref, gpost_ref, gpre_ref, xo_ref, h_ref):
    x_new = x_ref[...] + _rms(y_ref[...].astype(F32), gpost_ref[...])
    xo_ref[...] = x_new
    h_ref[...] = _rms(x_new, gpre_ref[...]).astype(h_ref.dtype)


def _resid_kernel(x_ref, y_ref, gpost_ref, xo_ref):
    xo_ref[...] = x_ref[...] + _rms(y_ref[...].astype(F32), gpost_ref[...])


def _resid_norm(x, y, g_post, g_pre_next, *, rows=NORM_ROWS):
    t, d = x.shape
    row_spec = pl.BlockSpec((rows, d), lambda i: (i, 0))
    vec_spec = pl.BlockSpec((1, d), lambda i: (0, 0))
    if g_pre_next is None:
        return pl.pallas_call(
            _resid_kernel,
            grid=(t // rows,),
            in_specs=[row_spec, row_spec, vec_spec],
            out_specs=row_spec,
            out_shape=jax.ShapeDtypeStruct((t, d), F32),
            compiler_params=_params(1),
            name="resid",
        )(x, y, g_post.reshape(1, d)), None
    return pl.pallas_call(
        _resid_norm_kernel,
        grid=(t // rows,),
        in_specs=[row_spec, row_spec, vec_spec, vec_spec],
        out_specs=[row_spec, row_spec],
        out_shape=[jax.ShapeDtypeStruct((t, d), F32), jax.ShapeDtypeStruct((t, d), BF16)],
        compiler_params=_params(1),
        name="resid_norm",
    )(x, y, g_post.reshape(1, d), g_pre_next.reshape(1, d))


def _dot_rounding_in_chunks(a_ref, w_ref, wbf_ref):
    kc = w_ref.shape[0] // W_CAST_CHUNKS
    acc = None
    for c in range(W_CAST_CHUNKS):
        ks = pl.ds(c * kc, kc)
        wbf_ref[ks, :] = w_ref[ks, :].astype(wbf_ref.dtype)
        part = jnp.dot(a_ref[:, ks], wbf_ref[ks, :], preferred_element_type=F32)
        acc = part if acc is None else acc + part
    return acc


def _with_rounded_weights(operands, emit):
    @pl.when(pl.program_id(1) == 0)
    def _first_token_tile():
        emit(*[_dot_rounding_in_chunks(*op) for op in operands])

    @pl.when(pl.program_id(1) != 0)
    def _later_token_tiles():
        emit(*[jnp.dot(a_ref[...], wbf_ref[...], preferred_element_type=F32)
               for a_ref, _, wbf_ref in operands])


def _mm_direct_kernel(a_ref, w_ref, o_ref):
    o_ref[...] = jnp.dot(a_ref[...], w_ref[...], preferred_element_type=F32).astype(o_ref.dtype)


def _mm_kernel(a_ref, w_ref, o_ref, wbf_ref):
    def emit(acc):
        o_ref[...] = acc.astype(o_ref.dtype)
    _with_rounded_weights([(a_ref, w_ref, wbf_ref)], emit)


def _mm_colscale_kernel(a_ref, w_ref, s_ref, o_ref, wbf_ref):
    def emit(acc):
        o_ref[...] = (acc * s_ref[...]).astype(o_ref.dtype)
    _with_rounded_weights([(a_ref, w_ref, wbf_ref)], emit)


def _matmul(a, w, layer, out_dtype, *, tile, name, col_scale=None, col_start=0):
    tm, tn = tile
    t, k = a.shape
    n = w.shape[2] - col_start
    c0 = col_start // tn
    assert col_start % tn == 0
    in_specs = [pl.BlockSpec((tm, k), lambda j, i: (i, 0)),
                pl.BlockSpec((None, k, tn), lambda j, i: (layer, 0, c0 + j))]
    operands = [a, w]
    if col_scale is not None:
        in_specs.append(pl.BlockSpec((1, tn), lambda j, i: (0, c0 + j)))
        operands.append(col_scale.reshape(1, -1))
    if w.dtype == BF16:
        assert col_scale is None
        body, scratch = _mm_direct_kernel, []
    else:
        body = _mm_kernel if col_scale is None else _mm_colscale_kernel
        scratch = [pltpu.VMEM((k, tn), BF16)]
    return pl.pallas_call(
        body,
        grid=(n // tn, t // tm),
        in_specs=in_specs,
        out_specs=pl.BlockSpec((tm, tn), lambda j, i: (i, j)),
        out_shape=jax.ShapeDtypeStruct((t, n), out_dtype),
        scratch_shapes=scratch,
        compiler_params=_params(2),
        name=name,
    )(*operands)


def _causal_conv3(u, halo, w_ref):
    rows, pad = u.shape[0], halo.shape[0]
    ext = jnp.concatenate([halo, u], axis=0)
    u1 = pltpu.roll(ext, 1, 0)[pad:pad + rows]
    u2 = pltpu.roll(ext, 2, 0)[pad:pad + rows]
    return w_ref[0:1, :] * u2 + w_ref[1:2, :] * u1 + w_ref[2:3, :] * u


def _conv_a_kernel(a_ref, win_ref, wb_ref, wc_ref, cw_ref, o_ref, winbf_ref, wbbf_ref, wcbf_ref,
                   halo_ref, *, tiles_per_seq):
    @pl.when(pl.program_id(1) % tiles_per_seq == 0)
    def _sequence_start():
        halo_ref[...] = jnp.zeros(halo_ref.shape, F32)

    def emit(a_in, a_b, a_c):
        u = a_c * a_in
        y = a_b * _causal_conv3(u, halo_ref[...], cw_ref)
        halo_ref[...] = u[u.shape[0] - halo_ref.shape[0]:, :]
        o_ref[...] = y.astype(o_ref.dtype)
    _with_rounded_weights([(a_ref, win_ref, winbf_ref), (a_ref, wb_ref, wbbf_ref),
                           (a_ref, wc_ref, wcbf_ref)], emit)


def _conv_a(h, w, layer, conv_w, *, seq, d_conv, tile=CONV_A_TILE):
    tm, tn = tile
    t, k = h.shape
    ncb = d_conv // tn
    assert seq % tm == 0
    section = lambda s: pl.BlockSpec((None, k, tn), lambda j, i: (layer, 0, s * ncb + j))
    return pl.pallas_call(
        functools.partial(_conv_a_kernel, tiles_per_seq=seq // tm),
        grid=(ncb, t // tm),
        in_specs=[pl.BlockSpec((tm, k), lambda j, i: (i, 0)), section(0), section(1), section(2),
                  pl.BlockSpec((CONV_WIDTH, tn), lambda j, i: (0, j))],
        out_specs=pl.BlockSpec((tm, tn), lambda j, i: (i, j)),
        out_shape=jax.ShapeDtypeStruct((t, d_conv), BF16),
        scratch_shapes=[pltpu.VMEM((k, tn), BF16)] * 3 + [pltpu.VMEM((F32_SUBLANES, tn), F32)],
        compiler_params=_params(2),
        name="conv_a",
    )(h, w, w, w, conv_w)


def _attn_kernel(slopes_ref, lq1_ref, lk1_ref, lq2_ref, lk2_ref, g_ref, q_ref, k_ref, v_ref,
                 o_ref, vt_ref, bias_ref, u_ref, p_ref, alpha_ref, m_ref, acc_ref,
                 *, lam_init, blk, seq):
    h = pl.program_id(1)
    slope2 = slopes_ref[h] * LOG2E
    nq = seq // blk
    assert nq >= 2

    def rows_of(j):
        if isinstance(j, int):
            return pl.ds(j * blk, blk)
        return pl.ds(pl.multiple_of(j * blk, blk), blk)

    def max_row_sq(ref, rows):
        x = ref[rows, :].astype(F32)
        sq = (x * x).reshape(-1, NORM_FOLD, F32_SUBLANES, x.shape[1])
        folded = jnp.max(sq, axis=1).reshape(-1, x.shape[1])
        per_map = jnp.maximum(jnp.sum(folded[:, :HEAD_DIM_QK], axis=1, keepdims=True),
                              jnp.sum(folded[:, HEAD_DIM_QK:], axis=1, keepdims=True))
        return jnp.max(per_map, axis=0, keepdims=True)

    def per_chunk(c, carry):
        rows = rows_of(c)
        vt_ref[0:HEAD_DIM_V, rows] = v_ref[rows, :].T
        return (jnp.maximum(carry[0], max_row_sq(q_ref, rows)),
                jnp.maximum(carry[1], max_row_sq(k_ref, rows)))
    zero_sq = jnp.zeros((1, 1), F32)
    q_sq, k_sq = lax.fori_loop(0, nq, per_chunk, (zero_sq, zero_sq))
    pad_row = lax.broadcasted_iota(jnp.int32, (BF16_SUBLANES, seq), 0)
    vt_ref[HEAD_DIM_V:, :] = jnp.where(pad_row == 0, 1.0, 0.0).astype(vt_ref.dtype)
    krel = lax.broadcasted_iota(jnp.int32, (blk, blk), 0)
    qrel = lax.broadcasted_iota(jnp.int32, (blk, blk), 1)
    bias_ref[0] = slope2 * krel.astype(F32)
    visible = (krel // CHUNK) <= (qrel // CHUNK)
    bias_ref[1] = jnp.where(visible, slope2 * (qrel - jnp.abs(qrel - krel)).astype(F32), NEG_INF)

    logit_bound = jnp.sqrt(q_sq) * jnp.sqrt(k_sq)
    far_keys = (2.0 * logit_bound + UNDERFLOW_LOG2) / slope2
    reach = jnp.max(jnp.minimum(jnp.floor(far_keys * (1.0 / blk)), float(nq))).astype(jnp.int32) + 1
    reach = jnp.minimum(reach, nq)
    n_items = (reach + 1) * (reach + 2) // 2 + (nq - 1 - reach) * (reach + 1)
    n_items = jnp.where(reach >= nq - 1, nq * (nq + 1) // 2, n_items)

    lam = (jnp.exp(jnp.sum(lq1_ref[...] * lk1_ref[...], axis=-1, keepdims=True))
           - jnp.exp(jnp.sum(lq2_ref[...] * lk2_ref[...], axis=-1, keepdims=True))
           + lam_init)
    out_gain = (g_ref[...] * (1.0 - lam_init)).T

    def first_block(qi):
        return jnp.maximum(qi - reach, 0)

    def next_item(item):
        qi, j = item
        last = j == qi
        return jnp.where(last, qi + 1, qi), jnp.where(last, first_block(qi + 1), j + 1)

    def scores(item, slot):
        qi, j = item
        k = k_ref[rows_of(j), :]
        q = q_ref[rows_of(qi), :]
        which = jnp.asarray(j == qi, jnp.int32)
        for c in range(2):
            kc = k[:, c * HEAD_DIM_QK:(c + 1) * HEAD_DIM_QK]
            qc = q[:, c * HEAD_DIM_QK:(c + 1) * HEAD_DIM_QK]
            u = lax.dot_general(kc, qc, (((1,), (1,)), ((), ())), preferred_element_type=F32)
            u_ref[slot, c] = u + bias_ref[which]

    def softmax(item, slot):
        qi, j = item
        off = slope2 * jnp.full((1, blk), (j - qi) * blk, jnp.int32).astype(F32)
        first = jnp.full((1, blk), j - first_block(qi), jnp.int32) == 0
        for c in range(2):
            m_old = jnp.where(first, NEG_INF, m_ref[c])
            m_new = jnp.maximum(m_old, jnp.max(u_ref[slot, c], axis=0, keepdims=True) + off)
            p_ref[slot, c] = jnp.exp2(u_ref[slot, c] - (m_new - off)).astype(p_ref.dtype)
            alpha_ref[slot, c] = jnp.exp2(m_old - m_new)
            m_ref[c] = m_new

    def values(item, slot):
        qi, j = item
        vt = vt_ref[:, rows_of(j)]
        half = qi % 2
        for c in range(2):
            pv = jnp.dot(vt, p_ref[slot, c], preferred_element_type=F32)
            acc_ref[half, c] = alpha_ref[slot, c] * acc_ref[half, c] + pv

    def finish_query_block(qi):
        half = qi % 2
        l0 = acc_ref[half, 0, HEAD_DIM_V:HEAD_DIM_V + 1, :]
        l1 = acc_ref[half, 1, HEAD_DIM_V:HEAD_DIM_V + 1, :]
        ot = (acc_ref[half, 0, 0:HEAD_DIM_V, :] * (1.0 / l0)
              - acc_ref[half, 1, 0:HEAD_DIM_V, :] * (lam / l1))
        ot = ot * lax.rsqrt(jnp.mean(ot * ot, axis=0, keepdims=True) + EPS)
        o_ref[rows_of(qi), :] = (ot * out_gain).astype(o_ref.dtype).T

    def is_last(item):
        return item[1] == item[0]

    def trip(items, slot):
        done, mid, ahead = items
        softmax(mid, 1 - slot)
        scores(ahead, slot)
        values(done, slot)
        return mid, ahead, next_item(ahead)

    def finish_if_last(*done_items):
        ended = functools.reduce(jnp.logical_or, [is_last(d) for d in done_items])
        which = done_items[-1][0]
        for d in reversed(done_items[:-1]):
            which = jnp.where(is_last(d), d[0], which)

        @pl.when(ended)
        def _():
            finish_query_block(which)

    def trip_pair(i, items):
        after_first = trip(items, 0)
        after_second = trip(after_first, 1)
        finish_if_last(items[0], after_first[0])
        return after_second

    acc_ref[...] = jnp.zeros(acc_ref.shape, F32)
    m_ref[...] = jnp.full(m_ref.shape, NEG_INF, F32)
    zero = jnp.int32(0)
    item0, item1 = (zero, zero), (zero + 1, zero)
    scores(item0, 0)
    scores(item1, 1)
    softmax(item0, 0)
    n_trips = n_items - 2
    items = lax.fori_loop(0, n_trips // 2, trip_pair, (item0, item1, next_item(item1)))

    @pl.when(n_trips % 2 == 1)
    def _():
        trip(items, 0)
        finish_if_last(items[0])

    last_slot = (n_items - 1) % 2
    values((nq - 1, nq - 2), 1 - last_slot)
    softmax((nq - 1, nq - 1), last_slot)
    values((nq - 1, nq - 1), last_slot)
    finish_query_block(nq - 1)


def _diff_attention(proj, lam_q1, lam_k1, lam_q2, lam_k2, subln_g, *, batch, seq, q_col, k_col,
                    v_col, lam_init, blk=ATTN_BLOCK):
    t = proj.shape[0]
    hw = 2 * HEAD_DIM_QK
    assert hw == HEAD_DIM_V and seq % blk == 0 and blk % CHUNK == 0
    vt_rows = HEAD_DIM_V + BF16_SUBLANES
    slopes = 2.0 ** (-8.0 * jnp.arange(1, N_HEADS + 1, dtype=F32) / N_HEADS)
    vec = lambda n: pl.BlockSpec((1, n), lambda b, h, s: (0, 0))
    head_cols = lambda col: pl.BlockSpec((seq, hw), lambda b, h, s: (b, col // hw + h))
    grid_spec = pltpu.PrefetchScalarGridSpec(
        num_scalar_prefetch=1,
        grid=(batch, N_HEADS),
        in_specs=[vec(HEAD_DIM_QK), vec(HEAD_DIM_QK), vec(HEAD_DIM_QK), vec(HEAD_DIM_QK),
                  vec(HEAD_DIM_V), head_cols(q_col), head_cols(k_col), head_cols(v_col)],
        out_specs=pl.BlockSpec((seq, HEAD_DIM_V), lambda b, h, s: (b, h)),
        scratch_shapes=[pltpu.VMEM((vt_rows, seq), BF16),
                        pltpu.VMEM((2, blk, blk), F32),
                        pltpu.VMEM((2, 2, blk, blk), F32),
                        pltpu.VMEM((2, 2, blk, blk), BF16),
                        pltpu.VMEM((2, 2, 1, blk), F32),
                        pltpu.VMEM((2, 1, blk), F32),
                        pltpu.VMEM((2, 2, vt_rows, blk), F32)],
    )
    return pl.pallas_call(
        functools.partial(_attn_kernel, lam_init=lam_init, blk=blk, seq=seq),
        grid_spec=grid_spec,
        out_shape=jax.ShapeDtypeStruct((t, N_HEADS * HEAD_DIM_V), BF16),
        compiler_params=_params(2),
        name="diff_attn",
    )(slopes, lam_q1.reshape(1, -1), lam_k1.reshape(1, -1), lam_q2.reshape(1, -1),
      lam_k2.reshape(1, -1), subln_g.reshape(1, -1), proj, proj, proj)


def _merge_kernel(ca_ref, at_ref, wa_ref, wb_ref, ga_ref, gb_ref, ba_ref, bb_ref, o_ref,
                  wabf_ref, wbbf_ref):
    def emit(y_a, y_b):
        g_a = jax.nn.sigmoid(ga_ref[...].astype(F32) + ba_ref[...])
        g_b = jax.nn.sigmoid(gb_ref[...].astype(F32) + bb_ref[...])
        o_ref[...] = (g_a * y_a + g_b * y_b).astype(o_ref.dtype)
    _with_rounded_weights([(ca_ref, wa_ref, wabf_ref), (at_ref, wb_ref, wbbf_ref)], emit)


def _merge(ca, attn, w_a, w_b, layer, proj, b_gate, *, gate_col, d_model, tile=MERGE_TILE):
    tm, tn = tile
    t, ka = ca.shape
    kb = attn.shape[1]
    gcb = gate_col // tn
    ncb = d_model // tn
    bias = b_gate.reshape(1, -1)
    return pl.pallas_call(
        _merge_kernel,
        grid=(ncb, t // tm),
        in_specs=[pl.BlockSpec((tm, ka), lambda j, i: (i, 0)),
                  pl.BlockSpec((tm, kb), lambda j, i: (i, 0)),
                  pl.BlockSpec((None, ka, tn), lambda j, i: (layer, 0, j)),
                  pl.BlockSpec((None, kb, tn), lambda j, i: (layer, 0, j)),
                  pl.BlockSpec((tm, tn), lambda j, i: (i, gcb + j)),
                  pl.BlockSpec((tm, tn), lambda j, i: (i, gcb + ncb + j)),
                  pl.BlockSpec((1, tn), lambda j, i: (0, j)),
                  pl.BlockSpec((1, tn), lambda j, i: (0, ncb + j))],
        out_specs=pl.BlockSpec((tm, tn), lambda j, i: (i, j)),
        out_shape=jax.ShapeDtypeStruct((t, d_model), BF16),
        scratch_shapes=[pltpu.VMEM((ka, tn), BF16), pltpu.VMEM((kb, tn), BF16)],
        compiler_params=_params(2),
        name="merge",
    )(ca, attn, w_a, w_b, proj, proj, bias, bias)


def _ffn_in_kernel(a_ref, wg_ref, wu_ref, cw_ref, o_ref, wgbf_ref, wubf_ref, halo_ref,
                   *, tiles_per_seq):
    @pl.when(pl.program_id(1) % tiles_per_seq == 0)
    def _sequence_start():
        halo_ref[...] = jnp.zeros(halo_ref.shape, F32)

    def emit(gate, up):
        z = _causal_conv3(gate, halo_ref[...], cw_ref)
        halo_ref[...] = gate[gate.shape[0] - halo_ref.shape[0]:, :]
        c0 = math.sqrt(2.0 / math.pi)
        act = 0.5 * z * (1.0 + jnp.tanh(c0 * (z + 0.044715 * (z * z * z))))
        o_ref[...] = (act * up).astype(o_ref.dtype)
    _with_rounded_weights([(a_ref, wg_ref, wgbf_ref), (a_ref, wu_ref, wubf_ref)], emit)


def _ffn_in(h, w, layer, conv_w, *, seq, d_ff, tile=FFN_IN_TILE):
    tm, tn = tile
    t, k = h.shape
    ncb = d_ff // tn
    assert seq % tm == 0
    return pl.pallas_call(
        functools.partial(_ffn_in_kernel, tiles_per_seq=seq // tm),
        grid=(ncb, t // tm),
        in_specs=[pl.BlockSpec((tm, k), lambda j, i: (i, 0)),
                  pl.BlockSpec((None, k, tn), lambda j, i: (layer, 0, j)),
                  pl.BlockSpec((None, k, tn), lambda j, i: (layer, 0, ncb + j)),
                  pl.BlockSpec((CONV_WIDTH, tn), lambda j, i: (0, j))],
        out_specs=pl.BlockSpec((tm, tn), lambda j, i: (i, j)),
        out_shape=jax.ShapeDtypeStruct((t, d_ff), BF16),
        scratch_shapes=[pltpu.VMEM((k, tn), BF16), pltpu.VMEM((k, tn), BF16),
                        pltpu.VMEM((F32_SUBLANES, tn), F32)],
        compiler_params=_params(2),
        name="ffn_in",
    )(h, w, w, conv_w)


def _lambda_init(layer_idx):
    return 0.8 - 0.6 * math.exp(-0.3 * layer_idx)


def kernel(x, w_in, b_gate, conv_a, w_a_out, lam_q1, lam_k1, lam_q2, lam_k2, subln_g, w_b_out, w_o,
           norm_mix_pre, norm_mix_post, w_ffn_in, conv_ffn, w_ffn_out, norm_ffn_pre, norm_ffn_post):
    batch, seq, d_model = x.shape
    depth = w_in.shape[0]
    d_conv = conv_a.shape[-1]
    d_ff = conv_ffn.shape[-1]
    d_qk = N_HEADS * 2 * HEAD_DIM_QK
    proj_start = 3 * d_conv
    q_col = 0
    k_col = q_col + d_qk
    v_col = k_col + d_qk
    gate_col = v_col + N_HEADS * HEAD_DIM_V

    d_in = w_in.shape[-1]
    q_scale = jnp.ones((d_in,), F32).at[proj_start + q_col:proj_start + k_col].set(QK_SCALE_LOG2)

    w_ffn_out_bf = w_ffn_out.astype(BF16)

    xt = x.reshape(batch * seq, d_model)
    h = _norm(xt, norm_mix_pre[0])
    for l in range(depth):
        ca = _conv_a(h, w_in, l, conv_a[l], seq=seq, d_conv=d_conv)
        proj = _matmul(h, w_in, l, BF16, tile=IN_PROJ_TILE, name="in_proj", col_scale=q_scale,
                       col_start=proj_start)
        attn = _diff_attention(proj, lam_q1[l], lam_k1[l], lam_q2[l], lam_k2[l], subln_g[l],
                               batch=batch, seq=seq, q_col=q_col, k_col=k_col, v_col=v_col,
                               lam_init=_lambda_init(l))
        merged = _merge(ca, attn, w_a_out, w_b_out, l, proj, b_gate[l],
                        gate_col=gate_col, d_model=d_model)
        mix = _matmul(merged, w_o, l, BF16, tile=W_O_TILE, name="w_o")
        xt, h2 = _resid_norm(xt, mix, norm_mix_post[l], norm_ffn_pre[l])

        f = _ffn_in(h2, w_ffn_in, l, conv_ffn[l], seq=seq, d_ff=d_ff)
        ffn = _matmul(f, w_ffn_out_bf, l, BF16, tile=FFN_OUT_TILE, name="ffn_out")
        g_next = norm_mix_pre[l + 1] if l + 1 < depth else None
        xt, h = _resid_norm(xt, ffn, norm_ffn_post[l], g_next)
    return xt.reshape(batch, seq, d_model)
```
